```python
import jax
import jax.numpy as jnp
from jax import lax
import numpy as np

D_MODEL = 4096
BATCH = 4
SEQ = 2048
DEPTH = 2
DEC_BATCH = 8
DEC_SEQ = 8
PAST_LEN = 16384
PAGE_SIZE = 128

N_HEADS = 16
HEAD_DIM = 128
D_ATTN = N_HEADS * HEAD_DIM
D_CONV = D_MODEL - D_ATTN
CONV_WIDTH = 31
CONV_STATE = CONV_WIDTH - 1
DILATED_PATTERNS = ((128, 1), (512, 4), (2048, 16))
MAX_WINDOW = 2048
ATTN_BLOCK = 128
D_IN = 3 * D_ATTN + 2 * D_CONV
N_MOD = 6
PEER_HEADS = 8
PEER_TOPK = 16
N_KEYS = 128
N_EXPERTS = N_KEYS * N_KEYS
PEER_QDIM = 256
PEER_HALF = PEER_QDIM // 2
PEER_BLOCK = 64
NORM_EPS = 1e-6

kernel_name = 'hybrid_dilated_conformer_peer_step'


def _rmsnorm(x, g):
    xf = x.astype(jnp.float32)
    y = xf * lax.rsqrt(jnp.mean(xf * xf, axis=-1, keepdims=True) + NORM_EPS)
    return (y * g.astype(jnp.float32)).astype(x.dtype)


def _layernorm(x, g, b):
    xf = x.astype(jnp.float32)
    mu = jnp.mean(xf, axis=-1, keepdims=True)
    xc = xf - mu
    y = xc * lax.rsqrt(jnp.mean(xc * xc, axis=-1, keepdims=True) + NORM_EPS)
    return (y * g.astype(jnp.float32) + b.astype(jnp.float32)).astype(x.dtype)


def _dilated_prompt(q, k, v, window, dilation):
    B, S, H, Dh = q.shape
    n = window // dilation
    L = S // dilation
    nb = -(-L // ATTN_BLOCK)
    Lp = nb * ATTN_BLOCK
    N = B * dilation

    def sub(t):
        t = t.reshape(B, L, dilation, H, Dh).transpose(0, 2, 1, 3, 4).reshape(N, L, H, Dh)
        return jnp.pad(t, ((0, 0), (0, Lp - L), (0, 0), (0, 0)))

    def band(t):
        tb = t.reshape(N, nb, ATTN_BLOCK, H, Dh)
        prev = jnp.pad(tb, ((0, 0), (1, 0), (0, 0), (0, 0), (0, 0)))[:, :-1]
        return jnp.concatenate([prev, tb], axis=2)

    qb = sub(q).reshape(N, nb, ATTN_BLOCK, H, Dh)
    kb = band(sub(k))
    vb = band(sub(v))
    s = jnp.einsum('nbqhd,nbkhd->nbhqk', qb, kb, preferred_element_type=jnp.float32) * (HEAD_DIM ** -0.5)
    qpos = jnp.arange(ATTN_BLOCK)[:, None] + ATTN_BLOCK
    kpos = jnp.arange(2 * ATTN_BLOCK)[None, :]
    dist = qpos - kpos
    in_band = (dist >= 0) & (dist <= n)
    has_prev = (jnp.arange(nb)[:, None, None] > 0) | (kpos[None] >= ATTN_BLOCK)
    mask = in_band[None] & has_prev
    s = jnp.where(mask[None, :, None], s, -jnp.inf)
    lse = jax.nn.logsumexp(s, axis=-1)
    p = jnp.exp(s - lse[..., None])
    o = jnp.einsum('nbhqk,nbkhd->nbqhd', p.astype(v.dtype), vb)
    o = o.reshape(N, Lp, H, Dh)[:, :L].reshape(B, dilation, L, H, Dh)
    o = o.transpose(0, 2, 1, 3, 4).reshape(B, S, H, Dh)
    lse = lse.transpose(0, 1, 3, 2).reshape(N, Lp, H)[:, :L].reshape(B, dilation, L, H)
    lse = lse.transpose(0, 2, 1, 3).reshape(B, S, H)
    return o, lse


def _dilated_sample(q, kc, vc, window, dilation):
    B, T, H, Dh = q.shape
    W = kc.shape[1] - T
    n = window // dilation
    idx = W + jnp.arange(T)[:, None] - dilation * jnp.arange(n + 1)[None, :]
    valid = idx >= 0
    idxc = jnp.maximum(idx, 0)
    kg = kc[:, idxc]
    vg = vc[:, idxc]
    s = jnp.einsum('bthd,btjhd->bthj', q, kg, preferred_element_type=jnp.float32) * (HEAD_DIM ** -0.5)
    s = jnp.where(valid[None, :, None, :], s, -jnp.inf)
    lse = jax.nn.logsumexp(s, axis=-1)
    p = jnp.exp(s - lse[..., None])
    o = jnp.einsum('bthj,btjhd->bthd', p.astype(vc.dtype), vg)
    return o, lse


def _mixture(parts):
    outs = jnp.stack([o for o, _ in parts])
    lses = jnp.stack([l for _, l in parts])
    alpha = jax.nn.softmax(lses, axis=0)
    return jnp.einsum('pbsh,pbshd->bshd', alpha.astype(outs.dtype), outs)


def _peer_block(tb, w_pq, sub_keys, peer_u, peer_v):
    T = tb.shape[0]
    q = (tb @ w_pq).reshape(T, PEER_HEADS, 2, PEER_HALF)
    s = jnp.einsum('thpc,hpnc->thpn', q, sub_keys, preferred_element_type=jnp.float32)
    s1, i1 = lax.top_k(s[:, :, 0], PEER_TOPK)
    s2, i2 = lax.top_k(s[:, :, 1], PEER_TOPK)
    cand = (s1[..., :, None] + s2[..., None, :]).reshape(T, PEER_HEADS, PEER_TOPK * PEER_TOPK)
    cidx = (i1[..., :, None] * N_KEYS + i2[..., None, :]).reshape(T, PEER_HEADS, PEER_TOPK * PEER_TOPK)
    top, pos = lax.top_k(cand, PEER_TOPK)
    idx = jnp.take_along_axis(cidx, pos, axis=-1)
    g = jax.nn.softmax(top, axis=-1)
    u = peer_u[idx]
    a = jax.nn.gelu(jnp.einsum('thkd,td->thk', u, tb, preferred_element_type=jnp.float32), approximate=False)
    wgt = (g * a).astype(tb.dtype)
    return jnp.einsum('thk,thkd->td', wgt, peer_v[idx])


def _peer(h, w_pq, sub_keys, peer_u, peer_v):
    shp = h.shape
    t = h.reshape(-1, D_MODEL)
    T = t.shape[0]
    Tp = -(-T // PEER_BLOCK) * PEER_BLOCK
    t = jnp.pad(t, ((0, Tp - T), (0, 0))).reshape(Tp // PEER_BLOCK, PEER_BLOCK, D_MODEL)
    out = lax.map(lambda tb: _peer_block(tb, w_pq, sub_keys, peer_u, peer_v), t)
    return out.reshape(Tp, D_MODEL)[:T].reshape(shp)


def _layer(x, c, past_k, past_v, past_conv, w_mod, b_mod, g_mix, g_ffn, w_in, w_dw, b_dw,
           g_cln, b_cln, w_out, w_pq, sub_keys, peer_u, peer_v):
    B, S, _ = x.shape
    mod = (jax.nn.silu(c) @ w_mod + b_mod).reshape(B, N_MOD, 1, D_MODEL)
    shift1, scale1, gate1 = mod[:, 0], mod[:, 1], mod[:, 2]
    shift2, scale2, gate2 = mod[:, 3], mod[:, 4], mod[:, 5]

    h = _rmsnorm(x, g_mix) * (1 + scale1) + shift1
    proj = h @ w_in
    q, k, v, ga, gb = jnp.split(proj, [D_ATTN, 2 * D_ATTN, 3 * D_ATTN, 3 * D_ATTN + D_CONV], axis=-1)
    q = q.reshape(B, S, N_HEADS, HEAD_DIM)
    k = k.reshape(B, S, N_HEADS, HEAD_DIM)
    v = v.reshape(B, S, N_HEADS, HEAD_DIM)
    if past_k is None:
        parts = [_dilated_prompt(q, k, v, w, d) for w, d in DILATED_PATTERNS]
        keep = min(MAX_WINDOW, S)
        k_rows, v_rows = k[:, S - keep:], v[:, S - keep:]
        past_conv = jnp.zeros((B, CONV_STATE, D_CONV), x.dtype)
    else:
        kc = jnp.concatenate([past_k, k], axis=1)
        vc = jnp.concatenate([past_v, v], axis=1)
        parts = [_dilated_sample(q, kc, vc, w, d) for w, d in DILATED_PATTERNS]
        k_rows, v_rows = k, v
    attn = _mixture(parts).reshape(B, S, D_ATTN)

    u = ga * jax.nn.sigmoid(gb)
    u_ext = jnp.concatenate([past_conv, u], axis=1)
    conv = lax.conv_general_dilated(u_ext, w_dw[:, None, :], (1,), 'VALID',
                                    dimension_numbers=('NWC', 'WIO', 'NWC'),
                                    feature_group_count=D_CONV) + b_dw
    conv = jax.nn.silu(_layernorm(conv, g_cln, b_cln))
    x = x + gate1 * (jnp.concatenate([attn, conv], axis=-1) @ w_out)

    h2 = _rmsnorm(x, g_ffn) * (1 + scale2) + shift2
    x = x + gate2 * _peer(h2, w_pq, sub_keys, peer_u, peer_v)
    return x, k_rows, v_rows, u_ext[:, u_ext.shape[1] - CONV_STATE:]


def setup_inputs(seed: int = 0) -> dict:
    key = jax.random.key(seed)
    ks = jax.random.split(key, 22)
    f32 = jnp.float32

    def nrm(k, shape, scale):
        return jax.random.normal(k, shape, f32) * scale

    wbuf = min(MAX_WINDOW, PAST_LEN)
    return {
        'x_prompt': nrm(ks[0], (BATCH, SEQ, D_MODEL), 1.0),
        'x_sample': nrm(ks[1], (DEC_BATCH, DEC_SEQ, D_MODEL), 1.0),
        'cache_k': nrm(ks[2], (DEPTH, DEC_BATCH, wbuf, N_HEADS, HEAD_DIM), 1.0),
        'cache_v': nrm(ks[3], (DEPTH, DEC_BATCH, wbuf, N_HEADS, HEAD_DIM), 1.0),
        'state_conv': nrm(ks[4], (DEPTH, DEC_BATCH, CONV_STATE, D_CONV), 0.5),
        'c_prompt': nrm(ks[5], (BATCH, D_MODEL), 1.0),
        'c_sample': nrm(ks[6], (DEC_BATCH, D_MODEL), 1.0),
        'w_mod': nrm(ks[7], (DEPTH, D_MODEL, N_MOD * D_MODEL), 0.5 * D_MODEL ** -0.5),
        'b_mod': nrm(ks[8], (DEPTH, N_MOD * D_MODEL), 0.02),
        'g_mix': 1.0 + nrm(ks[9], (DEPTH, D_MODEL), 0.02),
        'g_ffn': 1.0 + nrm(ks[10], (DEPTH, D_MODEL), 0.02),
        'w_in': nrm(ks[11], (DEPTH, D_MODEL, D_IN), D_MODEL ** -0.5),
        'w_dw': nrm(ks[12], (DEPTH, CONV_WIDTH, D_CONV), CONV_WIDTH ** -0.5),
        'b_dw': nrm(ks[13], (DEPTH, D_CONV), 0.02),
        'g_conv_ln': 1.0 + nrm(ks[14], (DEPTH, D_CONV), 0.02),
        'b_conv_ln': nrm(ks[15], (DEPTH, D_CONV), 0.02),
        'w_out': nrm(ks[16], (DEPTH, D_MODEL, D_MODEL), D_MODEL ** -0.5),
        'w_peer_q': nrm(ks[17], (DEPTH, D_MODEL, PEER_HEADS * PEER_QDIM), D_MODEL ** -0.5),
        'peer_sub_keys': nrm(ks[18], (DEPTH, PEER_HEADS, 2, N_KEYS, PEER_HALF), PEER_HALF ** -0.5),
        'peer_u': nrm(ks[19], (DEPTH, N_EXPERTS, D_MODEL), D_MODEL ** -0.5),
        'peer_v': nrm(ks[20], (DEPTH, N_EXPERTS, D_MODEL), 1.0),
        'g_final': 1.0 + nrm(ks[21], (D_MODEL,), 0.02),
    }


def reference(x_prompt, x_sample, cache_k, cache_v, state_conv, c_prompt, c_sample,
              w_mod, b_mod, g_mix, g_ffn, w_in, w_dw, b_dw, g_conv_ln, b_conv_ln, w_out,
              w_peer_q, peer_sub_keys, peer_u, peer_v, g_final):
    xp, xs = x_prompt, x_sample
    kp, vp, cp, ksm, vsm, csm = [], [], [], [], [], []
    for l in range(DEPTH):
        wts = (w_mod[l], b_mod[l], g_mix[l], g_ffn[l], w_in[l], w_dw[l], b_dw[l],
               g_conv_ln[l], b_conv_ln[l], w_out[l], w_peer_q[l], peer_sub_keys[l],
               peer_u[l], peer_v[l])
        xp, k1, v1, s1 = _layer(xp, c_prompt, None, None, None, *wts)
        xs, k2, v2, s2 = _layer(xs, c_sample, cache_k[l], cache_v[l], state_conv[l], *wts)
        kp.append(k1)
        vp.append(v1)
        cp.append(s1)
        ksm.append(k2)
        vsm.append(v2)
        csm.append(s2)
    y_prompt = _rmsnorm(xp, g_final)
    y_sample = _rmsnorm(xs, g_final)
    return (y_prompt, y_sample, jnp.stack(kp), jnp.stack(vp), jnp.stack(cp),
            jnp.stack(ksm), jnp.stack(vsm), jnp.stack(csm))
```

```python
import functools

import jax
import jax.numpy as jnp
from jax import lax
from jax.experimental import pallas as pl
from jax.experimental.pallas import tpu as pltpu

F32 = jnp.float32
BF16 = jnp.bfloat16

D_MODEL = 4096
N_HEADS = 16
HEAD_DIM = 128
D_ATTN = N_HEADS * HEAD_DIM
D_CONV = D_MODEL - D_ATTN
CONV_WIDTH = 31
CONV_STATE = CONV_WIDTH - 1
DILATIONS = (1, 4, 16)
ATTN_BLOCK = 128
D_IN = 3 * D_ATTN + 2 * D_CONV
N_MOD = 6
PEER_HEADS = 8
PEER_TOPK = 16
N_KEYS = 128
N_EXPERTS = N_KEYS * N_KEYS
PEER_QDIM = 256
NORM_EPS = 1e-6
ATTN_SCALE = HEAD_DIM ** -0.5

LANES = 128
STATE_PAD = 32
VMEM_LIMIT = 56 * 1024 * 1024
NEG_INF = float("-inf")


def _params(*sem):
    return pltpu.CompilerParams(dimension_semantics=sem, vmem_limit_bytes=VMEM_LIMIT)


def _mod_kernel(c_ref, w_ref, b_ref, o_ref):
    c = c_ref[...]
    a = (c * jax.nn.sigmoid(c)).astype(BF16)
    o_ref[...] = jnp.dot(a, w_ref[...].astype(BF16), preferred_element_type=F32) + b_ref[...]


def _modulation(c_all, w_mod, b_mod):
    depth, _, n = w_mod.shape
    r = c_all.shape[0]
    tn = 512
    return pl.pallas_call(
        _mod_kernel,
        grid=(depth, n // tn),
        in_specs=[pl.BlockSpec((r, D_MODEL), lambda l, j: (0, 0)),
                  pl.BlockSpec((None, D_MODEL, tn), lambda l, j: (l, 0, j)),
                  pl.BlockSpec((None, 1, tn), lambda l, j: (l, 0, j))],
        out_specs=pl.BlockSpec((None, r, tn), lambda l, j: (l, 0, j)),
        out_shape=jax.ShapeDtypeStruct((depth, r, n), F32),
        compiler_params=_params("parallel", "parallel"),
        name="modulation",
    )(c_all, w_mod, b_mod.reshape(depth, 1, n))


def _norm_mod_kernel(x_ref, g_ref, sc_ref, sh_ref, o_ref):
    x = x_ref[...]
    y = x * lax.rsqrt(jnp.mean(x * x, axis=-1, keepdims=True) + NORM_EPS) * g_ref[...]
    o_ref[...] = (y * (1.0 + sc_ref[...]) + sh_ref[...]).astype(o_ref.dtype)


def _norm_kernel(x_ref, g_ref, o_ref):
    x = x_ref[...]
    y = x * lax.rsqrt(jnp.mean(x * x, axis=-1, keepdims=True) + NORM_EPS) * g_ref[...]
    o_ref[...] = y.astype(o_ref.dtype)


def _row_spec(arr, tm, tn, tiles_per_group):
    r = arr.shape[1]
    rows = 1 if r == 1 else tm
    if r == 1:
        return pl.BlockSpec((None, rows, tn), lambda i, j: (i // tiles_per_group, 0, j))
    return pl.BlockSpec((None, rows, tn), lambda i, j: (0, i, j))


def _norm_mod(x, g, scale, shift, rows_per_group, tm, out_dtype):
    m = x.shape[0]
    tpg = max(rows_per_group // tm, 1)
    g2 = g.reshape(1, D_MODEL)
    return pl.pallas_call(
        _norm_mod_kernel,
        grid=(m // tm, 1),
        in_specs=[pl.BlockSpec((tm, D_MODEL), lambda i, j: (i, 0)),
                  pl.BlockSpec((1, D_MODEL), lambda i, j: (0, 0)),
                  _row_spec(scale, tm, D_MODEL, tpg),
                  _row_spec(shift, tm, D_MODEL, tpg)],
        out_specs=pl.BlockSpec((tm, D_MODEL), lambda i, j: (i, 0)),
        out_shape=jax.ShapeDtypeStruct((m, D_MODEL), out_dtype),
        compiler_params=_params("parallel", "arbitrary"),
        name="norm_mod",
    )(x, g2, scale, shift)


def _final_norm(x, g, tm):
    m = x.shape[0]
    return pl.pallas_call(
        _norm_kernel,
        grid=(m // tm,),
        in_specs=[pl.BlockSpec((tm, D_MODEL), lambda i: (i, 0)),
                  pl.BlockSpec((1, D_MODEL), lambda i: (0, 0))],
        out_specs=pl.BlockSpec((tm, D_MODEL), lambda i: (i, 0)),
        out_shape=jax.ShapeDtypeStruct((m, D_MODEL), F32),
        compiler_params=_params("parallel"),
        name="final_norm",
    )(x, g.reshape(1, D_MODEL))


def _mm_kernel(a_ref, w_ref, o_ref):
    acc = jnp.dot(a_ref[...].astype(BF16), w_ref[...].astype(BF16), preferred_element_type=F32)
    o_ref[...] = acc.astype(o_ref.dtype)


def _mm_res_kernel(a_ref, w_ref, x_ref, gate_ref, o_ref):
    acc = jnp.dot(a_ref[...].astype(BF16), w_ref[...].astype(BF16), preferred_element_type=F32)
    o_ref[...] = x_ref[...] + gate_ref[...] * acc


def _matmul(a, w, layer, tm, tn, out_dtype=F32, res=None, rows_per_group=None):
    m, k = a.shape
    n = w.shape[2]
    in_specs = [pl.BlockSpec((tm, k), lambda i, j: (i, 0)),
                pl.BlockSpec((None, k, tn), lambda i, j: (layer, 0, j))]
    args = [a, w]
    kern = _mm_kernel
    if res is not None:
        x, gate = res
        tpg = max(rows_per_group // tm, 1)
        in_specs += [pl.BlockSpec((tm, tn), lambda i, j: (i, j)), _row_spec(gate, tm, tn, tpg)]
        args += [x, gate]
        kern = _mm_res_kernel
    return pl.pallas_call(
        kern,
        grid=(m // tm, n // tn),
        in_specs=in_specs,
        out_specs=pl.BlockSpec((tm, tn), lambda i, j: (i, j)),
        out_shape=jax.ShapeDtypeStruct((m, n), out_dtype),
        compiler_params=_params("parallel", "arbitrary"),
        name="matmul",
    )(*args)


def _attn_prompt_kernel(q_ref, k_ref, v_ref, o_ref, o_sc, lse_sc):
    seq = q_ref.shape[0]
    row = lax.broadcasted_iota(jnp.int32, (ATTN_BLOCK, ATTN_BLOCK), 0)
    col = lax.broadcasted_iota(jnp.int32, (ATTN_BLOCK, ATTN_BLOCK), 1)
    own_mask = col <= row
    prev_mask = col >= row
    nt = (((1,), (1,)), ((), ()))

    for p, dil in enumerate(DILATIONS):
        length = seq // dil
        for r in range(dil):
            for blk in range(length // ATTN_BLOCK):
                start = r + blk * ATTN_BLOCK * dil
                rows = pl.ds(start, ATTN_BLOCK, stride=dil) if dil > 1 else pl.ds(start, ATTN_BLOCK)
                q = q_ref[rows, :].astype(BF16)
                k_own = k_ref[rows, :].astype(BF16)
                v_own = v_ref[rows, :].astype(BF16)
                s_own = lax.dot_general(q, k_own, nt, preferred_element_type=F32) * ATTN_SCALE
                s_own = jnp.where(own_mask, s_own, NEG_INF)
                m = jnp.max(s_own, axis=-1, keepdims=True)
                if blk > 0:
                    pstart = start - ATTN_BLOCK * dil
                    prows = pl.ds(pstart, ATTN_BLOCK, stride=dil) if dil > 1 else pl.ds(pstart, ATTN_BLOCK)
                    k_prev = k_ref[prows, :].astype(BF16)
                    v_prev = v_ref[prows, :].astype(BF16)
                    s_prev = lax.dot_general(q, k_prev, nt, preferred_element_type=F32) * ATTN_SCALE
                    s_prev = jnp.where(prev_mask, s_prev, NEG_INF)
                    m = jnp.maximum(m, jnp.max(s_prev, axis=-1, keepdims=True))
                p_own = jnp.exp(s_own - m)
                den = jnp.sum(p_own, axis=-1, keepdims=True)
                acc = jnp.dot(p_own.astype(BF16), v_own, preferred_element_type=F32)
                if blk > 0:
                    p_prev = jnp.exp(s_prev - m)
                    den = den + jnp.sum(p_prev, axis=-1, keepdims=True)
                    acc = acc + jnp.dot(p_prev.astype(BF16), v_prev, preferred_element_type=F32)
                o_sc[p, rows, :] = acc / den
                lse_sc[p, rows, :] = jnp.broadcast_to(m + jnp.log(den), (ATTN_BLOCK, HEAD_DIM))

    l0, l1, l2 = lse_sc[0], lse_sc[1], lse_sc[2]
    mx = jnp.maximum(jnp.maximum(l0, l1), l2)
    w0, w1, w2 = jnp.exp(l0 - mx), jnp.exp(l1 - mx), jnp.exp(l2 - mx)
    out = (w0 * o_sc[0] + w1 * o_sc[1] + w2 * o_sc[2]) / (w0 + w1 + w2)
    o_ref[...] = out.astype(o_ref.dtype)


def _attn_prompt(proj, batch, seq):
    blk = (seq, HEAD_DIM)
    return pl.pallas_call(
        _attn_prompt_kernel,
        grid=(batch, N_HEADS),
        in_specs=[pl.BlockSpec(blk, lambda b, h: (b, h)),
                  pl.BlockSpec(blk, lambda b, h: (b, N_HEADS + h)),
                  pl.BlockSpec(blk, lambda b, h: (b, 2 * N_HEADS + h))],
        out_specs=pl.BlockSpec(blk, lambda b, h: (b, h)),
        out_shape=jax.ShapeDtypeStruct((batch * seq, D_ATTN), BF16),
        scratch_shapes=[pltpu.VMEM((len(DILATIONS), seq, HEAD_DIM), F32),
                        pltpu.VMEM((len(DILATIONS), seq, HEAD_DIM), F32)],
        compiler_params=_params("parallel", "parallel"),
        name="attn_prompt",
    )(proj, proj, proj)


def _pattern_count(dist):
    cnt = jnp.zeros(dist.shape, F32)
    for dil in DILATIONS:
        ok = (dist >= 0) & ((dist & (dil - 1)) == 0) & (dist <= ATTN_BLOCK * dil)
        cnt = cnt + jnp.where(ok, 1.0, 0.0)
    return cnt


def _attn_sample_kernel(q_ref, kn_ref, vn_ref, kc_ref, vc_ref, o_ref, m_sc, l_sc, acc_sc, *, cache_len, chunk):
    c = pl.program_id(1)
    t_new = q_ref.shape[0]
    rows_n = t_new * N_HEADS
    nt = (((1,), (1,)), ((), ()))
    q = q_ref[...].reshape(rows_n, HEAD_DIM).astype(BF16)

    def scores(k2d, key_pos):
        n = k2d.shape[0]
        s = lax.dot_general(q, k2d.astype(BF16), nt, preferred_element_type=F32) * ATTN_SCALE
        ri = lax.broadcasted_iota(jnp.int32, (rows_n, n), 0)
        ci = lax.broadcasted_iota(jnp.int32, (rows_n, n), 1)
        same_head = (ri & (N_HEADS - 1)) == (ci & (N_HEADS - 1))
        dist = cache_len + (ri >> 4) - (key_pos + (ci >> 4))
        cnt = jnp.where(same_head, _pattern_count(dist), 0.0)
        return jnp.where(cnt > 0.0, s, NEG_INF), cnt

    @pl.when(c == 0)
    def _():
        s, cnt = scores(kn_ref[...].reshape(rows_n, HEAD_DIM), cache_len)
        m = jnp.max(s, axis=-1, keepdims=True)
        p = cnt * jnp.exp(s - m)
        m_sc[...] = m
        l_sc[...] = jnp.sum(p, axis=-1, keepdims=True)
        acc_sc[...] = jnp.dot(p.astype(BF16), vn_ref[...].reshape(rows_n, HEAD_DIM).astype(BF16),
                              preferred_element_type=F32)

    s, cnt = scores(kc_ref[...].reshape(chunk * N_HEADS, HEAD_DIM), c * chunk)
    m_old = m_sc[...]
    m_new = jnp.maximum(m_old, jnp.max(s, axis=-1, keepdims=True))
    alpha = jnp.exp(m_old - m_new)
    p = cnt * jnp.exp(s - m_new)
    l_sc[...] = alpha * l_sc[...] + jnp.sum(p, axis=-1, keepdims=True)
    acc_sc[...] = alpha * acc_sc[...] + jnp.dot(
        p.astype(BF16), vc_ref[...].reshape(chunk * N_HEADS, HEAD_DIM).astype(BF16), preferred_element_type=F32)
    m_sc[...] = m_new

    @pl.when(c == pl.num_programs(1) - 1)
    def _():
        o_ref[...] = (acc_sc[...] / l_sc[...]).reshape(t_new, N_HEADS, HEAD_DIM)


def _attn_sample(proj3, cache_k, cache_v, layer, batch, t_new):
    cache_len = cache_k.shape[2]
    chunk = 256
    rows_n = t_new * N_HEADS
    new_blk = (t_new, N_HEADS, HEAD_DIM)
    cache_blk = (None, None, chunk, N_HEADS, HEAD_DIM)
    return pl.pallas_call(
        functools.partial(_attn_sample_kernel, cache_len=cache_len, chunk=chunk),
        grid=(batch, cache_len // chunk),
        in_specs=[pl.BlockSpec(new_blk, lambda b, c: (b, 0, 0)),
                  pl.BlockSpec(new_blk, lambda b, c: (b, 1, 0)),
                  pl.BlockSpec(new_blk, lambda b, c: (b, 2, 0)),
                  pl.BlockSpec(cache_blk, lambda b, c: (layer, b, c, 0, 0)),
                  pl.BlockSpec(cache_blk, lambda b, c: (layer, b, c, 0, 0))],
        out_specs=pl.BlockSpec(new_blk, lambda b, c: (b, 0, 0)),
        out_shape=jax.ShapeDtypeStruct((batch * t_new, N_HEADS, HEAD_DIM), F32),
        scratch_shapes=[pltpu.VMEM((rows_n, 1), F32), pltpu.VMEM((rows_n, 1), F32),
                        pltpu.VMEM((rows_n, HEAD_DIM), F32)],
        compiler_params=_params("parallel", "arbitrary"),
        name="attn_sample",
    )(proj3, proj3, proj3, cache_k, cache_v)


def _conv_kernel(ga_ref, gb_ref, st_ref, wdw_ref, bdw_ref, g_ref, b_ref, o_ref, so_ref, ubuf, cbuf, *, ts, lane_chunk):
    j = pl.program_id(1)
    first = STATE_PAD - CONV_STATE

    @pl.when(j == 0)
    def _():
        ubuf[0:STATE_PAD, :] = st_ref[...]

    gb = gb_ref[...]
    ubuf[STATE_PAD:STATE_PAD + ts, :] = ga_ref[...] * jax.nn.sigmoid(gb)

    for c0 in range(0, D_CONV, lane_chunk):
        lanes = slice(c0, c0 + lane_chunk)
        acc = jnp.broadcast_to(bdw_ref[:, lanes], (ts, lane_chunk))
        for w in range(CONV_WIDTH):
            acc = acc + ubuf[first + w:first + w + ts, lanes] * wdw_ref[w:w + 1, lanes]
        cbuf[:, lanes] = acc

    conv = cbuf[...]
    mu = jnp.mean(conv, axis=-1, keepdims=True)
    xc = conv - mu
    y = xc * lax.rsqrt(jnp.mean(xc * xc, axis=-1, keepdims=True) + NORM_EPS)
    y = y * g_ref[...] + b_ref[...]
    o_ref[...] = (y * jax.nn.sigmoid(y)).astype(o_ref.dtype)

    @pl.when(j == pl.num_programs(1) - 1)
    def _():
        so_ref[...] = ubuf[ts + first:ts + STATE_PAD, :]

    tail = ubuf[ts:ts + STATE_PAD, :]
    ubuf[0:STATE_PAD, :] = tail


def _conv_module(proj, state_pad, w_dw, b_dw, g_ln, b_ln, layer, batch, seq, ts, out_dtype):
    steps = seq // ts
    ga_col = 3 * D_ATTN // D_CONV
    vec = lambda a: a.reshape(a.shape[0], 1, D_CONV)
    vspec = pl.BlockSpec((None, 1, D_CONV), lambda b, j: (layer, 0, 0))
    return pl.pallas_call(
        functools.partial(_conv_kernel, ts=ts, lane_chunk=512),
        grid=(batch, steps),
        in_specs=[pl.BlockSpec((ts, D_CONV), lambda b, j: (b * steps + j, ga_col)),
                  pl.BlockSpec((ts, D_CONV), lambda b, j: (b * steps + j, ga_col + 1)),
                  pl.BlockSpec((None, STATE_PAD, D_CONV), lambda b, j: (b, 0, 0)),
                  pl.BlockSpec((None, CONV_WIDTH, D_CONV), lambda b, j: (layer, 0, 0)),
                  vspec, vspec, vspec],
        out_specs=[pl.BlockSpec((ts, D_CONV), lambda b, j: (b * steps + j, 0)),
                   pl.BlockSpec((None, CONV_STATE, D_CONV), lambda b, j: (b, 0, 0))],
        out_shape=[jax.ShapeDtypeStruct((batch * seq, D_CONV), out_dtype),
                   jax.ShapeDtypeStruct((batch, CONV_STATE, D_CONV), F32)],
        scratch_shapes=[pltpu.VMEM((STATE_PAD + ts, D_CONV), F32), pltpu.VMEM((ts, D_CONV), F32)],
        compiler_params=_params("parallel", "arbitrary"),
        name="conv_module",
    )(proj, proj, state_pad, w_dw, vec(b_dw), vec(g_ln), vec(b_ln))


def _top16_rows(s_ref, val_ref, idx_ref, iota_f):
    n = s_ref.shape[0]

    def step(k, carry):
        s = s_ref[...]
        m = jnp.max(s, axis=0, keepdims=True)
        ix = jnp.min(jnp.where(s == m, iota_f, float(n)), axis=0, keepdims=True)
        val_ref[pl.ds(k, 1), :] = m
        idx_ref[pl.ds(k, 1), :] = ix
        s_ref[...] = jnp.where(iota_f == ix, NEG_INF, s)
        return carry

    lax.fori_loop(0, PEER_TOPK, step, 0)


def _peer_topk_kernel(q_ref, keys_ref, g_ref, s_sc, v1_sc, i1_sc, v2_sc, i2_sc, cand_sc, cid_sc, top_sc, pos_sc,
                      e_sc, w_sc, i1t_sc, i2t_sc, wt_sc):
    tm = q_ref.shape[0]
    nt = (((1,), (1,)), ((), ()))
    iota_keys = lax.broadcasted_iota(jnp.int32, (N_KEYS, tm), 0).astype(F32)
    iota_cand = lax.broadcasted_iota(jnp.int32, (PEER_TOPK * PEER_TOPK, tm), 0).astype(F32)

    def head(h, carry):
        col = pl.multiple_of(h * PEER_QDIM, PEER_QDIM)
        for half, (v_sc, i_sc) in enumerate(((v1_sc, i1_sc), (v2_sc, i2_sc))):
            qh = q_ref[:, pl.ds(col + half * N_KEYS, N_KEYS)].astype(BF16)
            kh = keys_ref[h, half].astype(BF16)
            s_sc[...] = lax.dot_general(kh, qh, nt, preferred_element_type=F32)
            _top16_rows(s_sc, v_sc, i_sc, iota_keys)
        v2 = v2_sc[...]
        i2 = i2_sc[...]
        for a in range(PEER_TOPK):
            rows = slice(a * PEER_TOPK, (a + 1) * PEER_TOPK)
            cand_sc[rows, :] = v1_sc[a:a + 1, :] + v2
            cid_sc[rows, :] = i1_sc[a:a + 1, :] * float(N_KEYS) + i2
        _top16_rows(cand_sc, top_sc, pos_sc, iota_cand)
        top = top_sc[...]
        pos = pos_sc[...]
        cid = cid_sc[...]
        ids = []
        for k in range(PEER_TOPK):
            ids.append(jnp.max(jnp.where(iota_cand == pos[k:k + 1, :], cid, -1.0), axis=0, keepdims=True))
        ex = jnp.exp(top - top[0:1, :])
        out_rows = pl.ds(pl.multiple_of(h * PEER_TOPK, PEER_TOPK), PEER_TOPK)
        e_sc[out_rows, :] = jnp.concatenate(ids, axis=0)
        w_sc[out_rows, :] = ex / jnp.sum(ex, axis=0, keepdims=True)
        return carry

    lax.fori_loop(0, PEER_HEADS, head, 0)

    e = e_sc[...]
    i1 = jnp.floor(e * (1.0 / N_KEYS))
    i1t_sc[...] = jnp.transpose(i1)
    i2t_sc[...] = jnp.transpose(e - i1 * float(N_KEYS))
    wt_sc[...] = jnp.transpose(w_sc[...])

    sub = lax.broadcasted_iota(jnp.int32, (N_KEYS, PEER_HEADS * PEER_TOPK), 0).astype(F32)

    def token(t, carry):
        i1row = i1t_sc[pl.ds(t, 1), :]
        i2row = i2t_sc[pl.ds(t, 1), :]
        wrow = wt_sc[pl.ds(t, 1), :]
        lmat = jnp.where(sub == i1row, wrow, 0.0).astype(BF16)
        rmat = jnp.where(sub == i2row, 1.0, 0.0).astype(BF16)
        g_ref[t] = lax.dot_general(lmat, rmat, nt, preferred_element_type=F32)
        return carry

    lax.fori_loop(0, tm, token, 0, unroll=2)


def _peer_topk(q, sub_keys, layer):
    t = q.shape[0]
    tm = LANES
    slots = PEER_HEADS * PEER_TOPK
    cands = PEER_TOPK * PEER_TOPK
    sc = lambda r: pltpu.VMEM((r, tm), F32)
    return pl.pallas_call(
        _peer_topk_kernel,
        grid=(t // tm,),
        in_specs=[pl.BlockSpec((tm, PEER_HEADS * PEER_QDIM), lambda i: (i, 0)),
                  pl.BlockSpec((None, PEER_HEADS, 2, N_KEYS, PEER_QDIM // 2), lambda i: (layer, 0, 0, 0, 0))],
        out_specs=pl.BlockSpec((tm, N_KEYS, N_KEYS), lambda i: (i, 0, 0)),
        out_shape=jax.ShapeDtypeStruct((t, N_KEYS, N_KEYS), F32),
        scratch_shapes=[sc(N_KEYS), sc(PEER_TOPK), sc(PEER_TOPK), sc(PEER_TOPK), sc(PEER_TOPK),
                        sc(cands), sc(cands), sc(PEER_TOPK), sc(PEER_TOPK),
                        sc(slots), sc(slots),
                        pltpu.VMEM((tm, slots), F32), pltpu.VMEM((tm, slots), F32), pltpu.VMEM((tm, slots), F32)],
        compiler_params=_params("parallel"),
        name="peer_topk",
    )(q, sub_keys)


def _gelu(x):
    return 0.5 * x * (1.0 + lax.erf(x * (2.0 ** -0.5)))


G_ROWS = 8


def _peer_up_kernel(h_ref, u_ref, g_ref, o_ref, *, nsub):
    nt = (((1,), (1,)), ((), ()))
    a = lax.dot_general(h_ref[...].astype(BF16), u_ref[...].astype(BF16), nt, preferred_element_type=F32)
    act = _gelu(a)
    phases = G_ROWS // nsub
    phase = pl.program_id(1) % phases
    for ph in range(phases):
        @pl.when(phase == ph)
        def _(ph=ph):
            for c in range(nsub):
                lanes = slice(c * N_KEYS, (c + 1) * N_KEYS)
                o_ref[:, lanes] = (g_ref[:, ph * nsub + c, :] * act[:, lanes]).astype(o_ref.dtype)


def _peer_up(h, peer_u, g3, layer, tm, tn):
    m = h.shape[0]
    nsub = tn // N_KEYS
    phases = G_ROWS // nsub
    return pl.pallas_call(
        functools.partial(_peer_up_kernel, nsub=nsub),
        grid=(m // tm, N_EXPERTS // tn),
        in_specs=[pl.BlockSpec((tm, D_MODEL), lambda i, j: (i, 0)),
                  pl.BlockSpec((None, tn, D_MODEL), lambda i, j: (layer, j, 0)),
                  pl.BlockSpec((tm, G_ROWS, N_KEYS), lambda i, j: (i, j // phases, 0))],
        out_specs=pl.BlockSpec((tm, tn), lambda i, j: (i, j)),
        out_shape=jax.ShapeDtypeStruct((m, N_EXPERTS), BF16),
        compiler_params=_params("parallel", "arbitrary"),
        name="peer_up",
    )(h, peer_u, g3)


def _peer_down_kernel(w_ref, v_ref, x_ref, gate_ref, o_ref):
    k = pl.program_id(2)

    @pl.when(k == 0)
    def _():
        o_ref[...] = jnp.zeros_like(o_ref)

    o_ref[...] += jnp.dot(w_ref[...], v_ref[...].astype(BF16), preferred_element_type=F32)

    @pl.when(k == pl.num_programs(2) - 1)
    def _():
        o_ref[...] = x_ref[...] + gate_ref[...] * o_ref[...]


def _peer_down(w, peer_v, x, gate, layer, rows_per_group, tm, tn, tk):
    m = w.shape[0]
    tpg = max(rows_per_group // tm, 1)
    r = gate.shape[1]
    if r == 1:
        gate_spec = pl.BlockSpec((None, 1, tn), lambda i, j, k: (i // tpg, 0, j))
    else:
        gate_spec = pl.BlockSpec((None, tm, tn), lambda i, j, k: (0, i, j))
    return pl.pallas_call(
        _peer_down_kernel,
        grid=(m // tm, D_MODEL // tn, N_EXPERTS // tk),
        in_specs=[pl.BlockSpec((tm, tk), lambda i, j, k: (i, k)),
                  pl.BlockSpec((None, tk, tn), lambda i, j, k: (layer, k, j)),
                  pl.BlockSpec((tm, tn), lambda i, j, k: (i, j)),
                  gate_spec],
        out_specs=pl.BlockSpec((tm, tn), lambda i, j, k: (i, j)),
        out_shape=jax.ShapeDtypeStruct((m, D_MODEL), F32),
        compiler_params=_params("parallel", "parallel", "arbitrary"),
        name="peer_down",
    )(w, peer_v, x, gate)


def _layer(x, mods, rows_per_group, tm, attn_fn, conv_fn, layer, weights, act_dtype):
    (g_mix, g_ffn, w_in, w_out, w_peer_q, peer_sub_keys, peer_u, peer_v) = weights
    shift1, scale1, gate1, shift2, scale2, gate2 = mods
    m = x.shape[0]

    h = _norm_mod(x, g_mix[layer], scale1, shift1, rows_per_group, min(tm, 512), act_dtype)
    proj = _matmul(h, w_in, layer, tm, 512)
    attn = attn_fn(proj)
    conv, conv_state = conv_fn(proj)
    mixed = jnp.concatenate([attn.astype(act_dtype), conv.astype(act_dtype)], axis=-1)
    x = _matmul(mixed, w_out, layer, tm, 512, res=(x, gate1), rows_per_group=rows_per_group)

    h2 = _norm_mod(x, g_ffn[layer], scale2, shift2, rows_per_group, min(tm, 512), act_dtype)
    q = _matmul(h2, w_peer_q, layer, tm, 512)
    pad = (-m) % LANES
    g3 = _peer_topk(jnp.pad(q, ((0, pad), (0, 0))) if pad else q, peer_sub_keys, layer)
    w = _peer_up(h2, peer_u, g3, layer, tm, 512)
    x = _peer_down(w, peer_v, x, gate2, layer, rows_per_group, tm, 1024, 1024)
    return x, proj, conv_state


def kernel(x_prompt, x_sample, cache_k, cache_v, state_conv, c_prompt, c_sample, w_mod, b_mod, g_mix, g_ffn, w_in,
           w_dw, b_dw, g_conv_ln, b_conv_ln, w_out, w_peer_q, peer_sub_keys, peer_u, peer_v, g_final):
    depth = w_mod.shape[0]
    batch, seq, _ = x_prompt.shape
    dbatch, dseq, _ = x_sample.shape
    mp, ms = batch * seq, dbatch * dseq

    c_all = jnp.concatenate([c_prompt, c_sample], axis=0)
    c_pad = (-c_all.shape[0]) % 8
    mod = _modulation(jnp.pad(c_all, ((0, c_pad), (0, 0))), w_mod, b_mod)

    xp = x_prompt.reshape(mp, D_MODEL)
    xs = x_sample.reshape(ms, D_MODEL)
    weights = (g_mix, g_ffn, w_in, w_out, w_peer_q, peer_sub_keys, peer_u, peer_v)
    zero_state = jnp.zeros((batch, STATE_PAD, D_CONV), F32)
    outs = {name: [] for name in ("kp", "vp", "cp", "ks", "vs", "cs")}

    for l in range(depth):
        mod_p = mod[l, :batch].reshape(batch, N_MOD, 1, D_MODEL)
        mods_p = tuple(mod_p[:, i] for i in range(N_MOD))
        mod_s = mod[l, batch:batch + dbatch].reshape(dbatch, N_MOD, D_MODEL)
        mods_s = tuple(jnp.repeat(mod_s[:, i], dseq, axis=0)[None] for i in range(N_MOD))

        conv_p = functools.partial(_conv_module, state_pad=zero_state, w_dw=w_dw, b_dw=b_dw, g_ln=g_conv_ln,
                                   b_ln=b_conv_ln, layer=l, batch=batch, seq=seq, ts=64, out_dtype=BF16)
        xp, proj_p, cstate_p = _layer(xp, mods_p, seq, 1024, functools.partial(_attn_prompt, batch=batch, seq=seq),
                                      conv_p, l, weights, BF16)

        state_s = jnp.pad(state_conv[l], ((0, 0), (STATE_PAD - CONV_STATE, 0), (0, 0)))
        conv_s = functools.partial(_conv_module, state_pad=state_s, w_dw=w_dw, b_dw=b_dw, g_ln=g_conv_ln,
                                   b_ln=b_conv_ln, layer=l, batch=dbatch, seq=dseq, ts=dseq, out_dtype=F32)

        def attn_s(proj, l=l):
            o = _attn_sample(proj.reshape(ms, D_IN // HEAD_DIM, HEAD_DIM), cache_k, cache_v, l, dbatch, dseq)
            return o.reshape(ms, D_ATTN)

        xs, proj_s, cstate_s = _layer(xs, mods_s, ms, ms, attn_s, conv_s, l, weights, F32)

        keep = min(cache_k.shape[2], seq)
        kv_p = proj_p.reshape(batch, seq, D_IN)[:, seq - keep:]
        outs["kp"].append(kv_p[..., D_ATTN:2 * D_ATTN].reshape(batch, keep, N_HEADS, HEAD_DIM))
        outs["vp"].append(kv_p[..., 2 * D_ATTN:3 * D_ATTN].reshape(batch, keep, N_HEADS, HEAD_DIM))
        outs["cp"].append(cstate_p)
        kv_s = proj_s.reshape(dbatch, dseq, D_IN)
        outs["ks"].append(kv_s[..., D_ATTN:2 * D_ATTN].reshape(dbatch, dseq, N_HEADS, HEAD_DIM))
        outs["vs"].append(kv_s[..., 2 * D_ATTN:3 * D_ATTN].reshape(dbatch, dseq, N_HEADS, HEAD_DIM))
        outs["cs"].append(cstate_s)

    y_prompt = _final_norm(xp, g_final, 512).reshape(batch, seq, D_MODEL)
    y_sample = _final_norm(xs, g_final, ms).reshape(dbatch, dseq, D_MODEL)
    return (y_prompt, y_sample, jnp.stack(outs["kp"]), jnp.stack(outs["vp"]), jnp.stack(outs["cp"]),
            jnp.stack(outs["ks"]), jnp.stack(outs["vs"]), jnp.stack(outs["cs"]))
```

```python
import functools

import jax
import jax.numpy as jnp
from jax import lax
from jax.experimental import pallas as pl
from jax.experimental.pallas import tpu as pltpu

F32 = jnp.float32
BF16 = jnp.bfloat16

D_MODEL = 4096
N_HEADS = 16
HEAD_DIM = 128
D_ATTN = N_HEADS * HEAD_DIM
D_CONV = D_MODEL - D_ATTN
CONV_WIDTH = 31
CONV_STATE = CONV_WIDTH - 1
DILATIONS = (1, 4, 16)
ATTN_BLOCK = 128
D_IN = 3 * D_ATTN + 2 * D_CONV
N_MOD = 6
PEER_HEADS = 8
PEER_TOPK = 16
N_KEYS = 128
N_EXPERTS = N_KEYS * N_KEYS
PEER_QDIM = 256
NORM_EPS = 1e-6
ATTN_SCALE = HEAD_DIM ** -0.5

LANES = 128
SUBLANES = 8
STATE_PAD = 32
VMEM_LIMIT = 56 * 1024 * 1024
NEG_INF = float("-inf")


def _params(*sem):
    return pltpu.CompilerParams(dimension_semantics=sem, vmem_limit_bytes=VMEM_LIMIT)


def _mod_kernel(c_ref, w_ref, b_ref, o_ref):
    c = c_ref[...]
    a = (c * jax.nn.sigmoid(c)).astype(BF16)
    o_ref[...] = jnp.dot(a, w_ref[...].astype(BF16), preferred_element_type=F32) + b_ref[...]


def _modulation(c_all, w_mod, b_mod):
    depth, _, n = w_mod.shape
    r = c_all.shape[0]
    tn = 512
    return pl.pallas_call(
        _mod_kernel,
        grid=(depth, n // tn),
        in_specs=[pl.BlockSpec((r, D_MODEL), lambda l, j: (0, 0)),
                  pl.BlockSpec((None, D_MODEL, tn), lambda l, j: (l, 0, j)),
                  pl.BlockSpec((None, 1, tn), lambda l, j: (l, 0, j))],
        out_specs=pl.BlockSpec((None, r, tn), lambda l, j: (l, 0, j)),
        out_shape=jax.ShapeDtypeStruct((depth, r, n), F32),
        compiler_params=_params("parallel", "parallel"),
        name="modulation",
    )(c_all, w_mod, b_mod.reshape(depth, 1, n))


def _norm_mod_kernel(x_ref, g_ref, sc_ref, sh_ref, o_ref):
    x = x_ref[...]
    y = x * lax.rsqrt(jnp.mean(x * x, axis=-1, keepdims=True) + NORM_EPS) * g_ref[...]
    o_ref[...] = (y * (1.0 + sc_ref[...]) + sh_ref[...]).astype(o_ref.dtype)


def _norm_kernel(x_ref, g_ref, o_ref):
    x = x_ref[...]
    y = x * lax.rsqrt(jnp.mean(x * x, axis=-1, keepdims=True) + NORM_EPS) * g_ref[...]
    o_ref[...] = y.astype(o_ref.dtype)


def _row_spec(arr, tm, tn, tiles_per_group):
    r = arr.shape[1]
    rows = 1 if r == 1 else tm
    if r == 1:
        return pl.BlockSpec((None, rows, tn), lambda i, j: (i // tiles_per_group, 0, j))
    return pl.BlockSpec((None, rows, tn), lambda i, j: (0, i, j))


def _norm_mod(x, g, scale, shift, rows_per_group, tm, out_dtype):
    m = x.shape[0]
    tpg = max(rows_per_group // tm, 1)
    g2 = g.reshape(1, D_MODEL)
    return pl.pallas_call(
        _norm_mod_kernel,
        grid=(m // tm, 1),
        in_specs=[pl.BlockSpec((tm, D_MODEL), lambda i, j: (i, 0)),
                  pl.BlockSpec((1, D_MODEL), lambda i, j: (0, 0)),
                  _row_spec(scale, tm, D_MODEL, tpg),
                  _row_spec(shift, tm, D_MODEL, tpg)],
        out_specs=pl.BlockSpec((tm, D_MODEL), lambda i, j: (i, 0)),
        out_shape=jax.ShapeDtypeStruct((m, D_MODEL), out_dtype),
        compiler_params=_params("parallel", "arbitrary"),
        name="norm_mod",
    )(x, g2, scale, shift)


def _final_norm(x, g, tm):
    m = x.shape[0]
    return pl.pallas_call(
        _norm_kernel,
        grid=(m // tm,),
        in_specs=[pl.BlockSpec((tm, D_MODEL), lambda i: (i, 0)),
                  pl.BlockSpec((1, D_MODEL), lambda i: (0, 0))],
        out_specs=pl.BlockSpec((tm, D_MODEL), lambda i: (i, 0)),
        out_shape=jax.ShapeDtypeStruct((m, D_MODEL), F32),
        compiler_params=_params("parallel"),
        name="final_norm",
    )(x, g.reshape(1, D_MODEL))


def _mm_kernel(a_ref, w_ref, o_ref):
    acc = jnp.dot(a_ref[...].astype(BF16), w_ref[...].astype(BF16), preferred_element_type=F32)
    o_ref[...] = acc.astype(o_ref.dtype)


def _mm_res_kernel(a_ref, w_ref, x_ref, gate_ref, o_ref):
    acc = jnp.dot(a_ref[...].astype(BF16), w_ref[...].astype(BF16), preferred_element_type=F32)
    o_ref[...] = x_ref[...] + gate_ref[...] * acc


def _matmul(a, w, layer, tm, tn, out_dtype=F32, res=None, rows_per_group=None):
    m, k = a.shape
    n = w.shape[2]
    in_specs = [pl.BlockSpec((tm, k), lambda i, j: (i, 0)),
                pl.BlockSpec((None, k, tn), lambda i, j: (layer, 0, j))]
    args = [a, w]
    kern = _mm_kernel
    if res is not None:
        x, gate = res
        tpg = max(rows_per_group // tm, 1)
        in_specs += [pl.BlockSpec((tm, tn), lambda i, j: (i, j)), _row_spec(gate, tm, tn, tpg)]
        args += [x, gate]
        kern = _mm_res_kernel
    return pl.pallas_call(
        kern,
        grid=(m // tm, n // tn),
        in_specs=in_specs,
        out_specs=pl.BlockSpec((tm, tn), lambda i, j: (i, j)),
        out_shape=jax.ShapeDtypeStruct((m, n), out_dtype),
        compiler_params=_params("parallel", "arbitrary"),
        name="matmul",
    )(*args)


def _mm2_res_kernel(a1_ref, a2_ref, w1_ref, w2_ref, x_ref, gate_ref, o_ref):
    acc = jnp.dot(a1_ref[...].astype(BF16), w1_ref[...].astype(BF16), preferred_element_type=F32)
    acc = acc + jnp.dot(a2_ref[...].astype(BF16), w2_ref[...].astype(BF16), preferred_element_type=F32)
    o_ref[...] = x_ref[...] + gate_ref[...] * acc


def _matmul_concat_res(a1, a2, w, layer, tm, tn, x, gate, rows_per_group):
    m, k1 = a1.shape
    k2 = a2.shape[1]
    assert k1 == k2 and w.shape[1] == k1 + k2
    n = w.shape[2]
    tpg = max(rows_per_group // tm, 1)
    return pl.pallas_call(
        _mm2_res_kernel,
        grid=(m // tm, n // tn),
        in_specs=[pl.BlockSpec((tm, k1), lambda i, j: (i, 0)),
                  pl.BlockSpec((tm, k2), lambda i, j: (i, 0)),
                  pl.BlockSpec((None, k1, tn), lambda i, j: (layer, 0, j)),
                  pl.BlockSpec((None, k2, tn), lambda i, j: (layer, 1, j)),
                  pl.BlockSpec((tm, tn), lambda i, j: (i, j)),
                  _row_spec(gate, tm, tn, tpg)],
        out_specs=pl.BlockSpec((tm, tn), lambda i, j: (i, j)),
        out_shape=jax.ShapeDtypeStruct((m, n), F32),
        compiler_params=_params("parallel", "arbitrary"),
        name="matmul_concat",
    )(a1, a2, w, w, x, gate)


ATTN_GROUP = 4


def _attn_prompt_kernel(q_ref, k_ref, v_ref, o_ref, o_sc, lse_sc):
    seq = q_ref.shape[0]
    nt = (((1,), (1,)), ((), ()))
    row = lax.broadcasted_iota(jnp.int32, (ATTN_BLOCK, 2 * ATTN_BLOCK), 0)
    col = lax.broadcasted_iota(jnp.int32, (ATTN_BLOCK, 2 * ATTN_BLOCK), 1)
    band_mask = (col >= row) & (col <= row + ATTN_BLOCK)
    causal_mask = (lax.broadcasted_iota(jnp.int32, (ATTN_BLOCK, ATTN_BLOCK), 1)
                   <= lax.broadcasted_iota(jnp.int32, (ATTN_BLOCK, ATTN_BLOCK), 0))

    def rows_of(start, n, dil):
        return pl.ds(start, n, stride=dil) if dil > 1 else pl.ds(start, n)

    def run_group(p, dil, units):
        qs, ks, vs, masks = [], [], [], []
        for start, has_prev in units:
            nk = 2 * ATTN_BLOCK if has_prev else ATTN_BLOCK
            kstart = start - ATTN_BLOCK * dil if has_prev else start
            qs.append(q_ref[rows_of(start, ATTN_BLOCK, dil), :].astype(BF16))
            ks.append(k_ref[rows_of(kstart, nk, dil), :].astype(BF16))
            v = v_ref[rows_of(kstart, nk, dil), :].astype(BF16)
            vs.append(jnp.concatenate([v, jnp.ones((nk, HEAD_DIM), BF16)], axis=-1))
            masks.append(band_mask if has_prev else causal_mask)
        ss = [jnp.where(mk, lax.dot_general(q, k, nt, preferred_element_type=F32) * ATTN_SCALE, NEG_INF)
              for q, k, mk in zip(qs, ks, masks)]
        ms = [jnp.max(s, axis=-1, keepdims=True) for s in ss]
        ps = [jnp.exp(s - m).astype(BF16) for s, m in zip(ss, ms)]
        rs = [jnp.dot(pr, v, preferred_element_type=F32) for pr, v in zip(ps, vs)]
        for (start, _), r, m in zip(units, rs, ms):
            acc, den = r[:, :HEAD_DIM], r[:, HEAD_DIM:]
            rows = rows_of(start, ATTN_BLOCK, dil)
            o_sc[p, rows, :] = acc / den
            lse_sc[p, rows, :] = m + jnp.log(den)

    for p, dil in enumerate(DILATIONS):
        nblk = seq // dil // ATTN_BLOCK
        units = [(r + blk * ATTN_BLOCK * dil, blk > 0) for blk in range(nblk) for r in range(dil)]
        for g0 in range(0, len(units), ATTN_GROUP):
            run_group(p, dil, units[g0:g0 + ATTN_GROUP])

    l0, l1, l2 = lse_sc[0], lse_sc[1], lse_sc[2]
    mx = jnp.maximum(jnp.maximum(l0, l1), l2)
    w0, w1, w2 = jnp.exp(l0 - mx), jnp.exp(l1 - mx), jnp.exp(l2 - mx)
    out = (w0 * o_sc[0] + w1 * o_sc[1] + w2 * o_sc[2]) / (w0 + w1 + w2)
    o_ref[...] = out.astype(o_ref.dtype)


def _attn_prompt(proj, batch, seq):
    blk = (seq, HEAD_DIM)
    return pl.pallas_call(
        _attn_prompt_kernel,
        grid=(batch, N_HEADS),
        in_specs=[pl.BlockSpec(blk, lambda b, h: (b, h)),
                  pl.BlockSpec(blk, lambda b, h: (b, N_HEADS + h)),
                  pl.BlockSpec(blk, lambda b, h: (b, 2 * N_HEADS + h))],
        out_specs=pl.BlockSpec(blk, lambda b, h: (b, h)),
        out_shape=jax.ShapeDtypeStruct((batch * seq, D_ATTN), BF16),
        scratch_shapes=[pltpu.VMEM((len(DILATIONS), seq, HEAD_DIM), F32),
                        pltpu.VMEM((len(DILATIONS), seq, HEAD_DIM), F32)],
        compiler_params=_params("parallel", "parallel"),
        name="attn_prompt",
    )(proj, proj, proj)


def _pattern_count(dist):
    cnt = jnp.zeros(dist.shape, F32)
    for dil in DILATIONS:
        ok = (dist >= 0) & ((dist & (dil - 1)) == 0) & (dist <= ATTN_BLOCK * dil)
        cnt = cnt + jnp.where(ok, 1.0, 0.0)
    return cnt


def _attn_sample_kernel(q_ref, kn_ref, vn_ref, kc_ref, vc_ref, o_ref, m_sc, l_sc, acc_sc, *, cache_len, chunk):
    c = pl.program_id(1)
    t_new = q_ref.shape[0]
    rows_n = t_new * N_HEADS
    nt = (((1,), (1,)), ((), ()))
    q = q_ref[...].reshape(rows_n, HEAD_DIM).astype(BF16)

    def scores(k2d, key_pos):
        n = k2d.shape[0]
        s = lax.dot_general(q, k2d.astype(BF16), nt, preferred_element_type=F32) * ATTN_SCALE
        ri = lax.broadcasted_iota(jnp.int32, (rows_n, n), 0)
        ci = lax.broadcasted_iota(jnp.int32, (rows_n, n), 1)
        same_head = (ri & (N_HEADS - 1)) == (ci & (N_HEADS - 1))
        dist = cache_len + (ri >> 4) - (key_pos + (ci >> 4))
        cnt = jnp.where(same_head, _pattern_count(dist), 0.0)
        return jnp.where(cnt > 0.0, s, NEG_INF), cnt

    @pl.when(c == 0)
    def _():
        s, cnt = scores(kn_ref[...].reshape(rows_n, HEAD_DIM), cache_len)
        m = jnp.max(s, axis=-1, keepdims=True)
        p = cnt * jnp.exp(s - m)
        m_sc[...] = m
        l_sc[...] = jnp.sum(p, axis=-1, keepdims=True)
        acc_sc[...] = jnp.dot(p.astype(BF16), vn_ref[...].reshape(rows_n, HEAD_DIM).astype(BF16),
                              preferred_element_type=F32)

    s, cnt = scores(kc_ref[...].reshape(chunk * N_HEADS, HEAD_DIM), c * chunk)
    m_old = m_sc[...]
    m_new = jnp.maximum(m_old, jnp.max(s, axis=-1, keepdims=True))
    alpha = jnp.exp(m_old - m_new)
    p = cnt * jnp.exp(s - m_new)
    l_sc[...] = alpha * l_sc[...] + jnp.sum(p, axis=-1, keepdims=True)
    acc_sc[...] = alpha * acc_sc[...] + jnp.dot(
        p.astype(BF16), vc_ref[...].reshape(chunk * N_HEADS, HEAD_DIM).astype(BF16), preferred_element_type=F32)
    m_sc[...] = m_new

    @pl.when(c == pl.num_programs(1) - 1)
    def _():
        o_ref[...] = (acc_sc[...] / l_sc[...]).reshape(t_new, N_HEADS, HEAD_DIM)


def _attn_sample(proj3, cache_k, cache_v, layer, batch, t_new):
    cache_len = cache_k.shape[2]
    chunk = 256
    rows_n = t_new * N_HEADS
    new_blk = (t_new, N_HEADS, HEAD_DIM)
    cache_blk = (None, None, chunk, N_HEADS, HEAD_DIM)
    return pl.pallas_call(
        functools.partial(_attn_sample_kernel, cache_len=cache_len, chunk=chunk),
        grid=(batch, cache_len // chunk),
        in_specs=[pl.BlockSpec(new_blk, lambda b, c: (b, 0, 0)),
                  pl.BlockSpec(new_blk, lambda b, c: (b, 1, 0)),
                  pl.BlockSpec(new_blk, lambda b, c: (b, 2, 0)),
                  pl.BlockSpec(cache_blk, lambda b, c: (layer, b, c, 0, 0)),
                  pl.BlockSpec(cache_blk, lambda b, c: (layer, b, c, 0, 0))],
        out_specs=pl.BlockSpec(new_blk, lambda b, c: (b, 0, 0)),
        out_shape=jax.ShapeDtypeStruct((batch * t_new, N_HEADS, HEAD_DIM), F32),
        scratch_shapes=[pltpu.VMEM((rows_n, 1), F32), pltpu.VMEM((rows_n, 1), F32),
                        pltpu.VMEM((rows_n, HEAD_DIM), F32)],
        compiler_params=_params("parallel", "arbitrary"),
        name="attn_sample",
    )(proj3, proj3, proj3, cache_k, cache_v)


def _conv_kernel(ga_ref, gb_ref, st_ref, wdw_ref, bdw_ref, g_ref, b_ref, o_ref, so_ref, ubuf, cbuf, *, ts, lane_chunk):
    j = pl.program_id(1)
    first = STATE_PAD - CONV_STATE

    @pl.when(j == 0)
    def _():
        ubuf[0:STATE_PAD, :] = st_ref[...]
        ubuf[STATE_PAD + ts:STATE_PAD + ts + SUBLANES, :] = jnp.zeros((SUBLANES, D_CONV), F32)

    gb = gb_ref[...]
    ubuf[STATE_PAD:STATE_PAD + ts, :] = ga_ref[...] * jax.nn.sigmoid(gb)

    for c0 in range(0, D_CONV, lane_chunk):
        lanes = slice(c0, c0 + lane_chunk)
        acc = jnp.broadcast_to(bdw_ref[:, lanes], (ts, lane_chunk))
        for rho in range(SUBLANES):
            part = None
            for base in range(0, STATE_PAD + SUBLANES, SUBLANES):
                w = base + rho - first
                if 0 <= w < CONV_WIDTH:
                    term = ubuf[base:base + ts + SUBLANES, lanes] * wdw_ref[w:w + 1, lanes]
                    part = term if part is None else part + term
            acc = acc + part[rho:rho + ts, :]
        cbuf[:, lanes] = acc

    conv = cbuf[...]
    mu = jnp.mean(conv, axis=-1, keepdims=True)
    xc = conv - mu
    y = xc * lax.rsqrt(jnp.mean(xc * xc, axis=-1, keepdims=True) + NORM_EPS)
    y = y * g_ref[...] + b_ref[...]
    o_ref[...] = (y * jax.nn.sigmoid(y)).astype(o_ref.dtype)

    @pl.when(j == pl.num_programs(1) - 1)
    def _():
        so_ref[...] = ubuf[ts + first:ts + STATE_PAD, :]

    tail = ubuf[ts:ts + STATE_PAD, :]
    ubuf[0:STATE_PAD, :] = tail


def _conv_module(proj, state_pad, w_dw, b_dw, g_ln, b_ln, layer, batch, seq, ts, out_dtype):
    steps = seq // ts
    ga_col = 3 * D_ATTN // D_CONV
    vec = lambda a: a.reshape(a.shape[0], 1, D_CONV)
    vspec = pl.BlockSpec((None, 1, D_CONV), lambda b, j: (layer, 0, 0))
    return pl.pallas_call(
        functools.partial(_conv_kernel, ts=ts, lane_chunk=512),
        grid=(batch, steps),
        in_specs=[pl.BlockSpec((ts, D_CONV), lambda b, j: (b * steps + j, ga_col)),
                  pl.BlockSpec((ts, D_CONV), lambda b, j: (b * steps + j, ga_col + 1)),
                  pl.BlockSpec((None, STATE_PAD, D_CONV), lambda b, j: (b, 0, 0)),
                  pl.BlockSpec((None, CONV_WIDTH, D_CONV), lambda b, j: (layer, 0, 0)),
                  vspec, vspec, vspec],
        out_specs=[pl.BlockSpec((ts, D_CONV), lambda b, j: (b * steps + j, 0)),
                   pl.BlockSpec((None, CONV_STATE, D_CONV), lambda b, j: (b, 0, 0))],
        out_shape=[jax.ShapeDtypeStruct((batch * seq, D_CONV), out_dtype),
                   jax.ShapeDtypeStruct((batch, CONV_STATE, D_CONV), F32)],
        scratch_shapes=[pltpu.VMEM((STATE_PAD + ts + SUBLANES, D_CONV), F32), pltpu.VMEM((ts, D_CONV), F32)],
        compiler_params=_params("parallel", "arbitrary"),
        name="conv_module",
    )(proj, proj, state_pad, w_dw, vec(b_dw), vec(g_ln), vec(b_ln))


CAND_PAD = 1.0e9
TOKEN_UNROLL = 16


def _cand_plan():
    tiles = [("a", 0, 0), ("a", 0, SUBLANES), ("a", 1, 0), ("b", 0, SUBLANES)] + [("b", b, 0) for b in range(5)]
    flat, seen = [], set()
    for kind, fixed, start in tiles:
        for r in range(SUBLANES):
            a, b = (fixed, start + r) if kind == "a" else (start + r, fixed)
            if (a + 1) * (b + 1) <= PEER_TOPK and (a, b) not in seen:
                seen.add((a, b))
                flat.append(float(a * PEER_TOPK + b))
            else:
                flat.append(CAND_PAD)
    need = {(a, b) for a in range(PEER_TOPK) for b in range(PEER_TOPK) if (a + 1) * (b + 1) <= PEER_TOPK}
    assert seen == need
    return tiles, flat


def _peer_topk_kernel(q_ref, keys_ref, flat_ref, g_ref, s_sc, v_sc, i_sc, cand_sc, cid_sc, top_sc,
                      e_sc, w_sc, i1t_sc, i2t_sc, wt_sc, *, tiles):
    tm = q_ref.shape[0]
    nt = (((1,), (1,)), ((), ()))
    iota_keys = lax.broadcasted_iota(jnp.int32, (N_KEYS, tm), 0).astype(F32)
    flat = flat_ref[...]
    valid = flat < CAND_PAD

    def head_pair(hp, carry):
        for hh in range(2):
            h = hp * 2 + hh
            col = pl.multiple_of(h * PEER_QDIM, PEER_QDIM)
            for half in range(2):
                qh = q_ref[:, pl.ds(col + half * N_KEYS, N_KEYS)].astype(BF16)
                kh = keys_ref[h, half].astype(BF16)
                s_sc[hh * 2 + half] = lax.dot_general(kh, qh, nt, preferred_element_type=F32)

        def key_step(k, c):
            for ch in range(4):
                s = s_sc[ch]
                m = jnp.max(s, axis=0, keepdims=True)
                ix = jnp.min(jnp.where(s == m, iota_keys, float(N_KEYS)), axis=0, keepdims=True)
                v_sc[ch, pl.ds(k, 1), :] = m
                i_sc[ch, pl.ds(k, 1), :] = ix
                s_sc[ch] = jnp.where(iota_keys == ix, NEG_INF, s)
            return c

        lax.fori_loop(0, PEER_TOPK, key_step, 0)

        for hh in range(2):
            v1, v2, i1, i2 = v_sc[2 * hh], v_sc[2 * hh + 1], i_sc[2 * hh], i_sc[2 * hh + 1]
            sums, ids = [], []
            for kind, fixed, start in tiles:
                one, rng = slice(fixed, fixed + 1), slice(start, start + SUBLANES)
                ra, rb = (one, rng) if kind == "a" else (rng, one)
                sums.append(v1[ra, :] + v2[rb, :])
                ids.append(i1[ra, :] * float(N_KEYS) + i2[rb, :])
            cand_sc[hh] = jnp.where(valid, jnp.concatenate(sums, axis=0), NEG_INF)
            cid_sc[hh] = jnp.concatenate(ids, axis=0)

        def cand_step(k, c):
            for hh in range(2):
                cnd = cand_sc[hh]
                m = jnp.max(cnd, axis=0, keepdims=True)
                pos = jnp.min(jnp.where(cnd == m, flat, 2.0 * CAND_PAD), axis=0, keepdims=True)
                hit = flat == pos
                top_sc[hh, pl.ds(k, 1), :] = m
                e_sc[pl.ds((hp * 2 + hh) * PEER_TOPK + k, 1), :] = jnp.max(
                    jnp.where(hit, cid_sc[hh], -1.0), axis=0, keepdims=True)
                cand_sc[hh] = jnp.where(hit, NEG_INF, cnd)
            return c

        lax.fori_loop(0, PEER_TOPK, cand_step, 0)

        for hh in range(2):
            top = top_sc[hh]
            ex = jnp.exp(top - top[0:1, :])
            out_rows = pl.ds(pl.multiple_of((hp * 2 + hh) * PEER_TOPK, PEER_TOPK), PEER_TOPK)
            w_sc[out_rows, :] = ex / jnp.sum(ex, axis=0, keepdims=True)
        return carry

    lax.fori_loop(0, PEER_HEADS // 2, head_pair, 0)

    e = e_sc[...]
    i1 = jnp.floor(e * (1.0 / N_KEYS))
    i1t_sc[...] = jnp.transpose(i1)
    i2t_sc[...] = jnp.transpose(e - i1 * float(N_KEYS))
    wt_sc[...] = jnp.transpose(w_sc[...])

    sub = lax.broadcasted_iota(jnp.int32, (N_KEYS, PEER_HEADS * PEER_TOPK), 0).astype(F32)

    def token(t, carry):
        i1row = i1t_sc[pl.ds(t, 1), :]
        i2row = i2t_sc[pl.ds(t, 1), :]
        wrow = wt_sc[pl.ds(t, 1), :]
        lmat = jnp.where(sub == i1row, wrow, 0.0).astype(BF16)
        rmat = jnp.where(sub == i2row, 1.0, 0.0).astype(BF16)
        g_ref[t] = lax.dot_general(lmat, rmat, nt, preferred_element_type=F32)
        return carry

    lax.fori_loop(0, tm, token, 0, unroll=TOKEN_UNROLL)


def _peer_topk(q, sub_keys, layer):
    t = q.shape[0]
    tm = LANES
    slots = PEER_HEADS * PEER_TOPK
    tiles, flat = _cand_plan()
    ncand = len(flat)
    flat_arr = jnp.broadcast_to(jnp.asarray(flat, F32)[:, None], (ncand, tm))
    sc = lambda *shape: pltpu.VMEM(shape + (tm,), F32)
    return pl.pallas_call(
        functools.partial(_peer_topk_kernel, tiles=tiles),
        grid=(t // tm,),
        in_specs=[pl.BlockSpec((tm, PEER_HEADS * PEER_QDIM), lambda i: (i, 0)),
                  pl.BlockSpec((None, PEER_HEADS, 2, N_KEYS, PEER_QDIM // 2), lambda i: (layer, 0, 0, 0, 0)),
                  pl.BlockSpec((ncand, tm), lambda i: (0, 0))],
        out_specs=pl.BlockSpec((tm, N_KEYS, N_KEYS), lambda i: (i, 0, 0)),
        out_shape=jax.ShapeDtypeStruct((t, N_KEYS, N_KEYS), F32),
        scratch_shapes=[sc(4, N_KEYS), sc(4, PEER_TOPK), sc(4, PEER_TOPK),
                        sc(2, ncand), sc(2, ncand), sc(2, PEER_TOPK),
                        sc(slots), sc(slots),
                        pltpu.VMEM((tm, slots), F32), pltpu.VMEM((tm, slots), F32), pltpu.VMEM((tm, slots), F32)],
        compiler_params=_params("parallel"),
        name="peer_topk",
    )(q, sub_keys, flat_arr)


def _gelu(x):
    return 0.5 * x * (1.0 + lax.erf(x * (2.0 ** -0.5)))


G_ROWS = 8


def _peer_up_kernel(h_ref, u_ref, g_ref, o_ref, *, nsub):
    nt = (((1,), (1,)), ((), ()))
    a = lax.dot_general(h_ref[...].astype(BF16), u_ref[...].astype(BF16), nt, preferred_element_type=F32)
    act = _gelu(a)
    tm = h_ref.shape[0]
    g2 = g_ref.reshape(tm * G_ROWS, N_KEYS)
    phases = G_ROWS // nsub
    phase = pl.program_id(1) % phases
    for ph in range(phases):
        @pl.when(phase == ph)
        def _(ph=ph):
            for c in range(nsub):
                lanes = slice(c * N_KEYS, (c + 1) * N_KEYS)
                g = g2[pl.ds(ph * nsub + c, tm, stride=G_ROWS), :]
                o_ref[:, lanes] = (g * act[:, lanes]).astype(o_ref.dtype)


def _peer_up(h, peer_u, g3, layer, tm, tn):
    m = h.shape[0]
    nsub = tn // N_KEYS
    phases = G_ROWS // nsub
    return pl.pallas_call(
        functools.partial(_peer_up_kernel, nsub=nsub),
        grid=(m // tm, N_EXPERTS // tn),
        in_specs=[pl.BlockSpec((tm, D_MODEL), lambda i, j: (i, 0)),
                  pl.BlockSpec((None, tn, D_MODEL), lambda i, j: (layer, j, 0)),
                  pl.BlockSpec((tm, G_ROWS, N_KEYS), lambda i, j: (i, j // phases, 0))],
        out_specs=pl.BlockSpec((tm, tn), lambda i, j: (i, j)),
        out_shape=jax.ShapeDtypeStruct((m, N_EXPERTS), BF16),
        compiler_params=_params("parallel", "arbitrary"),
        name="peer_up",
    )(h, peer_u, g3)


def _peer_down_kernel(w_ref, v_ref, x_ref, gate_ref, o_ref):
    k = pl.program_id(2)

    @pl.when(k == 0)
    def _():
        o_ref[...] = jnp.zeros_like(o_ref)

    o_ref[...] += jnp.dot(w_ref[...], v_ref[...].astype(BF16), preferred_element_type=F32)

    @pl.when(k == pl.num_programs(2) - 1)
    def _():
        o_ref[...] = x_ref[...] + gate_ref[...] * o_ref[...]


def _peer_down(w, peer_v, x, gate, layer, rows_per_group, tm, tn, tk):
    m = w.shape[0]
    tpg = max(rows_per_group // tm, 1)
    r = gate.shape[1]
    if r == 1:
        gate_spec = pl.BlockSpec((None, 1, tn), lambda i, j, k: (i // tpg, 0, j))
    else:
        gate_spec = pl.BlockSpec((None, tm, tn), lambda i, j, k: (0, i, j))
    return pl.pallas_call(
        _peer_down_kernel,
        grid=(m // tm, D_MODEL // tn, N_EXPERTS // tk),
        in_specs=[pl.BlockSpec((tm, tk), lambda i, j, k: (i, k)),
                  pl.BlockSpec((None, tk, tn), lambda i, j, k: (layer, k, j)),
                  pl.BlockSpec((tm, tn), lambda i, j, k: (i, j)),
                  gate_spec],
        out_specs=pl.BlockSpec((tm, tn), lambda i, j, k: (i, j)),
        out_shape=jax.ShapeDtypeStruct((m, D_MODEL), F32),
        compiler_params=_params("parallel", "parallel", "arbitrary"),
        name="peer_down",
    )(w, peer_v, x, gate)


def _layer(x, mods, rows_per_group, tm, attn_fn, conv_fn, layer, weights, act_dtype):
    (g_mix, g_ffn, w_in, w_out, w_peer_q, peer_sub_keys, peer_u, peer_v) = weights
    shift1, scale1, gate1, shift2, scale2, gate2 = mods
    m = x.shape[0]

    h = _norm_mod(x, g_mix[layer], scale1, shift1, rows_per_group, min(tm, 512), act_dtype)
    proj = _matmul(h, w_in, layer, tm, 512)
    attn = attn_fn(proj)
    conv, conv_state = conv_fn(proj)
    x = _matmul_concat_res(attn, conv, w_out, layer, tm, 512, x, gate1, rows_per_group)

    h2 = _norm_mod(x, g_ffn[layer], scale2, shift2, rows_per_group, min(tm, 512), act_dtype)
    q = _matmul(h2, w_peer_q, layer, tm, 512)
    pad = (-m) % LANES
    g3 = _peer_topk(jnp.pad(q, ((0, pad), (0, 0))) if pad else q, peer_sub_keys, layer)
    w = _peer_up(h2, peer_u, g3, layer, tm, 512)
    x = _peer_down(w, peer_v, x, gate2, layer, rows_per_group, tm, 1024, 2048)
    return x, proj, conv_state


def kernel(x_prompt, x_sample, cache_k, cache_v, state_conv, c_prompt, c_sample, w_mod, b_mod, g_mix, g_ffn, w_in,
           w_dw, b_dw, g_conv_ln, b_conv_ln, w_out, w_peer_q, peer_sub_keys, peer_u, peer_v, g_final):
    depth = w_mod.shape[0]
    batch, seq, _ = x_prompt.shape
    dbatch, dseq, _ = x_sample.shape
    mp, ms = batch * seq, dbatch * dseq

    c_all = jnp.concatenate([c_prompt, c_sample], axis=0)
    c_pad = (-c_all.shape[0]) % 8
    mod = _modulation(jnp.pad(c_all, ((0, c_pad), (0, 0))), w_mod, b_mod)

    xp = x_prompt.reshape(mp, D_MODEL)
    xs = x_sample.reshape(ms, D_MODEL)
    weights = (g_mix, g_ffn, w_in, w_out, w_peer_q, peer_sub_keys, peer_u, peer_v)
    zero_state = jnp.zeros((batch, STATE_PAD, D_CONV), F32)
    outs = {name: [] for name in ("kp", "vp", "cp", "ks", "vs", "cs")}

    for l in range(depth):
        mod_p = mod[l, :batch].reshape(batch, N_MOD, 1, D_MODEL)
        mods_p = tuple(mod_p[:, i] for i in range(N_MOD))
        mod_s = mod[l, batch:batch + dbatch].reshape(dbatch, N_MOD, D_MODEL)
        mods_s = tuple(jnp.repeat(mod_s[:, i], dseq, axis=0)[None] for i in range(N_MOD))

        conv_p = functools.partial(_conv_module, state_pad=zero_state, w_dw=w_dw, b_dw=b_dw, g_ln=g_conv_ln,
                                   b_ln=b_conv_ln, layer=l, batch=batch, seq=seq, ts=64, out_dtype=BF16)
        xp, proj_p, cstate_p = _layer(xp, mods_p, seq, 1024, functools.partial(_attn_prompt, batch=batch, seq=seq),
                                      conv_p, l, weights, BF16)

        state_s = jnp.pad(state_conv[l], ((0, 0), (STATE_PAD - CONV_STATE, 0), (0, 0)))
        conv_s = functools.partial(_conv_module, state_pad=state_s, w_dw=w_dw, b_dw=b_dw, g_ln=g_conv_ln,
                                   b_ln=b_conv_ln, layer=l, batch=dbatch, seq=dseq, ts=dseq, out_dtype=F32)

        def attn_s(proj, l=l):
            o = _attn_sample(proj.reshape(ms, D_IN // HEAD_DIM, HEAD_DIM), cache_k, cache_v, l, dbatch, dseq)
            return o.reshape(ms, D_ATTN)

        xs, proj_s, cstate_s = _layer(xs, mods_s, ms, ms, attn_s, conv_s, l, weights, F32)

        keep = min(cache_k.shape[2], seq)
        kv_p = proj_p.reshape(batch, seq, D_IN)[:, seq - keep:]
        outs["kp"].append(kv_p[..., D_ATTN:2 * D_ATTN].reshape(batch, keep, N_HEADS, HEAD_DIM))
        outs["vp"].append(kv_p[..., 2 * D_ATTN:3 * D_ATTN].reshape(batch, keep, N_HEADS, HEAD_DIM))
        outs["cp"].append(cstate_p)
        kv_s = proj_s.reshape(dbatch, dseq, D_IN)
        outs["ks"].append(kv_s[..., D_ATTN:2 * D_ATTN].reshape(dbatch, dseq, N_HEADS, HEAD_DIM))
        outs["vs"].append(kv_s[..., 2 * D_ATTN:3 * D_ATTN].reshape(dbatch, dseq, N_HEADS, HEAD_DIM))
        outs["cs"].append(cstate_s)

    y_prompt = _final_norm(xp, g_final, 512).reshape(batch, seq, D_MODEL)
    y_sample = _final_norm(xs, g_final, ms).reshape(dbatch, dseq, D_MODEL)
    return (y_prompt, y_sample, jnp.stack(outs["kp"]), jnp.stack(outs["vp"]), jnp.stack(outs["cp"]),
            jnp.stack(outs["ks"]), jnp.stack(outs["vs"]), jnp.stack(outs["cs"]))
```

```python
import functools

import jax
import jax.numpy as jnp
from jax import lax
from jax.experimental import pallas as pl
from jax.experimental.pallas import tpu as pltpu

F32 = jnp.float32
BF16 = jnp.bfloat16

D_MODEL = 4096
N_HEADS = 16
HEAD_DIM = 128
D_ATTN = N_HEADS * HEAD_DIM
D_CONV = D_MODEL - D_ATTN
CONV_WIDTH = 31
CONV_STATE = CONV_WIDTH - 1
DILATIONS = (1, 4, 16)
ATTN_BLOCK = 128
D_IN = 3 * D_ATTN + 2 * D_CONV
N_MOD = 6
PEER_HEADS = 8
PEER_TOPK = 16
N_KEYS = 128
N_EXPERTS = N_KEYS * N_KEYS
PEER_QDIM = 256
NORM_EPS = 1e-6
ATTN_SCALE = HEAD_DIM ** -0.5

LANES = 128
SUBLANES = 8
STATE_PAD = 32
VMEM_LIMIT = 56 * 1024 * 1024
NEG_INF = float("-inf")


def _params(*sem):
    return pltpu.CompilerParams(dimension_semantics=sem, vmem_limit_bytes=VMEM_LIMIT)


def _mod_kernel(c_ref, w_ref, b_ref, o_ref):
    c = c_ref[...]
    a = (c * jax.nn.sigmoid(c)).astype(BF16)
    o_ref[...] = jnp.dot(a, w_ref[...].astype(BF16), preferred_element_type=F32) + b_ref[...]


def _modulation(c_all, w_mod, b_mod):
    depth, _, n = w_mod.shape
    r = c_all.shape[0]
    tn = 512
    return pl.pallas_call(
        _mod_kernel,
        grid=(depth, n // tn),
        in_specs=[pl.BlockSpec((r, D_MODEL), lambda l, j: (0, 0)),
                  pl.BlockSpec((None, D_MODEL, tn), lambda l, j: (l, 0, j)),
                  pl.BlockSpec((None, 1, tn), lambda l, j: (l, 0, j))],
        out_specs=pl.BlockSpec((None, r, tn), lambda l, j: (l, 0, j)),
        out_shape=jax.ShapeDtypeStruct((depth, r, n), F32),
        compiler_params=_params("parallel", "parallel"),
        name="modulation",
    )(c_all, w_mod, b_mod.reshape(depth, 1, n))


def _norm_mod_kernel(x_ref, g_ref, sc_ref, sh_ref, o_ref):
    x = x_ref[...]
    y = x * lax.rsqrt(jnp.mean(x * x, axis=-1, keepdims=True) + NORM_EPS) * g_ref[...]
    o_ref[...] = (y * (1.0 + sc_ref[...]) + sh_ref[...]).astype(o_ref.dtype)


def _norm_kernel(x_ref, g_ref, o_ref):
    x = x_ref[...]
    y = x * lax.rsqrt(jnp.mean(x * x, axis=-1, keepdims=True) + NORM_EPS) * g_ref[...]
    o_ref[...] = y.astype(o_ref.dtype)


def _row_spec(arr, tm, tn, tiles_per_group):
    r = arr.shape[1]
    rows = 1 if r == 1 else tm
    if r == 1:
        return pl.BlockSpec((None, rows, tn), lambda i, j: (i // tiles_per_group, 0, j))
    return pl.BlockSpec((None, rows, tn), lambda i, j: (0, i, j))


def _norm_mod(x, g, scale, shift, rows_per_group, tm, out_dtype):
    m = x.shape[0]
    tpg = max(rows_per_group // tm, 1)
    g2 = g.reshape(1, D_MODEL)
    return pl.pallas_call(
        _norm_mod_kernel,
        grid=(m // tm, 1),
        in_specs=[pl.BlockSpec((tm, D_MODEL), lambda i, j: (i, 0)),
                  pl.BlockSpec((1, D_MODEL), lambda i, j: (0, 0)),
                  _row_spec(scale, tm, D_MODEL, tpg),
                  _row_spec(shift, tm, D_MODEL, tpg)],
        out_specs=pl.BlockSpec((tm, D_MODEL), lambda i, j: (i, 0)),
        out_shape=jax.ShapeDtypeStruct((m, D_MODEL), out_dtype),
        compiler_params=_params("parallel", "arbitrary"),
        name="norm_mod",
    )(x, g2, scale, shift)


def _final_norm(x, g, tm):
    m = x.shape[0]
    return pl.pallas_call(
        _norm_kernel,
        grid=(m // tm,),
        in_specs=[pl.BlockSpec((tm, D_MODEL), lambda i: (i, 0)),
                  pl.BlockSpec((1, D_MODEL), lambda i: (0, 0))],
        out_specs=pl.BlockSpec((tm, D_MODEL), lambda i: (i, 0)),
        out_shape=jax.ShapeDtypeStruct((m, D_MODEL), F32),
        compiler_params=_params("parallel"),
        name="final_norm",
    )(x, g.reshape(1, D_MODEL))


def _mm_kernel(a_ref, w_ref, o_ref):
    acc = jnp.dot(a_ref[...].astype(BF16), w_ref[...].astype(BF16), preferred_element_type=F32)
    o_ref[...] = acc.astype(o_ref.dtype)


def _mm_res_kernel(a_ref, w_ref, x_ref, gate_ref, o_ref):
    acc = jnp.dot(a_ref[...].astype(BF16), w_ref[...].astype(BF16), preferred_element_type=F32)
    o_ref[...] = x_ref[...] + gate_ref[...] * acc


def _matmul(a, w, layer, tm, tn, out_dtype=F32, res=None, rows_per_group=None):
    m, k = a.shape
    n = w.shape[2]
    in_specs = [pl.BlockSpec((tm, k), lambda i, j: (i, 0)),
                pl.BlockSpec((None, k, tn), lambda i, j: (layer, 0, j))]
    args = [a, w]
    kern = _mm_kernel
    if res is not None:
        x, gate = res
        tpg = max(rows_per_group // tm, 1)
        in_specs += [pl.BlockSpec((tm, tn), lambda i, j: (i, j)), _row_spec(gate, tm, tn, tpg)]
        args += [x, gate]
        kern = _mm_res_kernel
    return pl.pallas_call(
        kern,
        grid=(m // tm, n // tn),
        in_specs=in_specs,
        out_specs=pl.BlockSpec((tm, tn), lambda i, j: (i, j)),
        out_shape=jax.ShapeDtypeStruct((m, n), out_dtype),
        compiler_params=_params("parallel", "arbitrary"),
        name="matmul",
    )(*args)


def _in_proj_kernel(a_ref, w_ref, o_ref, k_ref, v_ref, *, kv_tiles):
    j = pl.program_id(1)
    acc = jnp.dot(a_ref[...].astype(BF16), w_ref[...].astype(BF16), preferred_element_type=F32)
    o_ref[...] = acc

    @pl.when((j >= kv_tiles) & (j < 2 * kv_tiles))
    def _():
        k_ref[...] = acc

    @pl.when((j >= 2 * kv_tiles) & (j < 3 * kv_tiles))
    def _():
        v_ref[...] = acc


def _in_proj(a, w, layer, tm, tn):
    m, k = a.shape
    n = w.shape[2]
    kv_tiles = D_ATTN // tn

    def kv_spec(first):
        return pl.BlockSpec((tm, tn), lambda i, j: (i, jnp.clip(j - first, 0, kv_tiles - 1)))

    return pl.pallas_call(
        functools.partial(_in_proj_kernel, kv_tiles=kv_tiles),
        grid=(m // tm, n // tn),
        in_specs=[pl.BlockSpec((tm, k), lambda i, j: (i, 0)),
                  pl.BlockSpec((None, k, tn), lambda i, j: (layer, 0, j))],
        out_specs=[pl.BlockSpec((tm, tn), lambda i, j: (i, j)), kv_spec(kv_tiles), kv_spec(2 * kv_tiles)],
        out_shape=[jax.ShapeDtypeStruct((m, n), F32), jax.ShapeDtypeStruct((m, D_ATTN), F32),
                   jax.ShapeDtypeStruct((m, D_ATTN), F32)],
        compiler_params=_params("parallel", "arbitrary"),
        name="in_proj",
    )(a, w)


def _mm2_res_kernel(a1_ref, a2_ref, w1_ref, w2_ref, x_ref, gate_ref, o_ref):
    acc = jnp.dot(a1_ref[...].astype(BF16), w1_ref[...].astype(BF16), preferred_element_type=F32)
    acc = acc + jnp.dot(a2_ref[...].astype(BF16), w2_ref[...].astype(BF16), preferred_element_type=F32)
    o_ref[...] = x_ref[...] + gate_ref[...] * acc


def _matmul_concat_res(a1, a2, w, layer, tm, tn, x, gate, rows_per_group):
    m, k1 = a1.shape
    k2 = a2.shape[1]
    assert k1 == k2 and w.shape[1] == k1 + k2
    n = w.shape[2]
    tpg = max(rows_per_group // tm, 1)
    return pl.pallas_call(
        _mm2_res_kernel,
        grid=(m // tm, n // tn),
        in_specs=[pl.BlockSpec((tm, k1), lambda i, j: (i, 0)),
                  pl.BlockSpec((tm, k2), lambda i, j: (i, 0)),
                  pl.BlockSpec((None, k1, tn), lambda i, j: (layer, 0, j)),
                  pl.BlockSpec((None, k2, tn), lambda i, j: (layer, 1, j)),
                  pl.BlockSpec((tm, tn), lambda i, j: (i, j)),
                  _row_spec(gate, tm, tn, tpg)],
        out_specs=pl.BlockSpec((tm, tn), lambda i, j: (i, j)),
        out_shape=jax.ShapeDtypeStruct((m, n), F32),
        compiler_params=_params("parallel", "arbitrary"),
        name="matmul_concat",
    )(a1, a2, w, w, x, gate)


ATTN_GROUP = 4


def _attn_prompt_kernel(q_ref, k_ref, v_ref, o_ref, o_sc, lse_sc):
    seq = q_ref.shape[0]
    nt = (((1,), (1,)), ((), ()))
    row = lax.broadcasted_iota(jnp.int32, (ATTN_BLOCK, 2 * ATTN_BLOCK), 0)
    col = lax.broadcasted_iota(jnp.int32, (ATTN_BLOCK, 2 * ATTN_BLOCK), 1)
    band_mask = (col >= row) & (col <= row + ATTN_BLOCK)
    causal_mask = (lax.broadcasted_iota(jnp.int32, (ATTN_BLOCK, ATTN_BLOCK), 1)
                   <= lax.broadcasted_iota(jnp.int32, (ATTN_BLOCK, ATTN_BLOCK), 0))

    def rows_of(start, n, dil):
        return pl.ds(start, n, stride=dil) if dil > 1 else pl.ds(start, n)

    def run_group(p, dil, units):
        qs, ks, vs, masks = [], [], [], []
        for start, has_prev in units:
            nk = 2 * ATTN_BLOCK if has_prev else ATTN_BLOCK
            kstart = start - ATTN_BLOCK * dil if has_prev else start
            qs.append(q_ref[rows_of(start, ATTN_BLOCK, dil), :].astype(BF16))
            ks.append(k_ref[rows_of(kstart, nk, dil), :].astype(BF16))
            v = v_ref[rows_of(kstart, nk, dil), :].astype(BF16)
            vs.append(jnp.concatenate([v, jnp.ones((nk, HEAD_DIM), BF16)], axis=-1))
            masks.append(band_mask if has_prev else causal_mask)
        ss = [jnp.where(mk, lax.dot_general(q, k, nt, preferred_element_type=F32) * ATTN_SCALE, NEG_INF)
              for q, k, mk in zip(qs, ks, masks)]
        ms = [jnp.max(s, axis=-1, keepdims=True) for s in ss]
        ps = [jnp.exp(s - m).astype(BF16) for s, m in zip(ss, ms)]
        rs = [jnp.dot(pr, v, preferred_element_type=F32) for pr, v in zip(ps, vs)]
        for (start, _), r, m in zip(units, rs, ms):
            acc, den = r[:, :HEAD_DIM], r[:, HEAD_DIM:]
            rows = rows_of(start, ATTN_BLOCK, dil)
            o_sc[p, rows, :] = acc / den
            lse_sc[p, rows, :] = m + jnp.log(den)

    for p, dil in enumerate(DILATIONS):
        nblk = seq // dil // ATTN_BLOCK
        units = [(r + blk * ATTN_BLOCK * dil, blk > 0) for blk in range(nblk) for r in range(dil)]
        for g0 in range(0, len(units), ATTN_GROUP):
            run_group(p, dil, units[g0:g0 + ATTN_GROUP])

    l0, l1, l2 = lse_sc[0], lse_sc[1], lse_sc[2]
    mx = jnp.maximum(jnp.maximum(l0, l1), l2)
    w0, w1, w2 = jnp.exp(l0 - mx), jnp.exp(l1 - mx), jnp.exp(l2 - mx)
    out = (w0 * o_sc[0] + w1 * o_sc[1] + w2 * o_sc[2]) / (w0 + w1 + w2)
    o_ref[...] = out.astype(o_ref.dtype)


def _attn_prompt(proj, batch, seq):
    blk = (seq, HEAD_DIM)
    return pl.pallas_call(
        _attn_prompt_kernel,
        grid=(batch, N_HEADS),
        in_specs=[pl.BlockSpec(blk, lambda b, h: (b, h)),
                  pl.BlockSpec(blk, lambda b, h: (b, N_HEADS + h)),
                  pl.BlockSpec(blk, lambda b, h: (b, 2 * N_HEADS + h))],
        out_specs=pl.BlockSpec(blk, lambda b, h: (b, h)),
        out_shape=jax.ShapeDtypeStruct((batch * seq, D_ATTN), BF16),
        scratch_shapes=[pltpu.VMEM((len(DILATIONS), seq, HEAD_DIM), F32),
                        pltpu.VMEM((len(DILATIONS), seq, HEAD_DIM), F32)],
        compiler_params=_params("parallel", "parallel"),
        name="attn_prompt",
    )(proj, proj, proj)


def _pattern_count(dist):
    cnt = jnp.zeros(dist.shape, F32)
    for dil in DILATIONS:
        ok = (dist >= 0) & ((dist & (dil - 1)) == 0) & (dist <= ATTN_BLOCK * dil)
        cnt = cnt + jnp.where(ok, 1.0, 0.0)
    return cnt


def _attn_sample_kernel(q_ref, kn_ref, vn_ref, kc_ref, vc_ref, o_ref, m_sc, l_sc, acc_sc, cnt_sc, *, cache_len, chunk):
    b = pl.program_id(0)
    c = pl.program_id(1)
    t_new = q_ref.shape[0]
    rows_n = t_new * N_HEADS
    nt = (((1,), (1,)), ((), ()))
    q = q_ref[...].reshape(rows_n, HEAD_DIM).astype(BF16)

    def pattern_weights(n, key_pos):
        ri = lax.broadcasted_iota(jnp.int32, (rows_n, n), 0)
        ci = lax.broadcasted_iota(jnp.int32, (rows_n, n), 1)
        same_head = (ri & (N_HEADS - 1)) == (ci & (N_HEADS - 1))
        dist = cache_len + (ri >> 4) - (key_pos + (ci >> 4))
        return jnp.where(same_head, _pattern_count(dist), 0.0)

    def scores(k2d, cnt):
        s = lax.dot_general(q, k2d.astype(BF16), nt, preferred_element_type=F32) * ATTN_SCALE
        return jnp.where(cnt > 0.0, s, NEG_INF)

    @pl.when(b == 0)
    def _():
        cnt_sc[c] = pattern_weights(chunk * N_HEADS, c * chunk)

    @pl.when(c == 0)
    def _():
        cnt = pattern_weights(rows_n, cache_len)
        s = scores(kn_ref[...].reshape(rows_n, HEAD_DIM), cnt)
        m = jnp.max(s, axis=-1, keepdims=True)
        p = cnt * jnp.exp(s - m)
        m_sc[...] = m
        l_sc[...] = jnp.sum(p, axis=-1, keepdims=True)
        acc_sc[...] = jnp.dot(p.astype(BF16), vn_ref[...].reshape(rows_n, HEAD_DIM).astype(BF16),
                              preferred_element_type=F32)

    cnt = cnt_sc[c]
    s = scores(kc_ref[...].reshape(chunk * N_HEADS, HEAD_DIM), cnt)
    m_old = m_sc[...]
    m_new = jnp.maximum(m_old, jnp.max(s, axis=-1, keepdims=True))
    alpha = jnp.exp(m_old - m_new)
    p = cnt * jnp.exp(s - m_new)
    l_sc[...] = alpha * l_sc[...] + jnp.sum(p, axis=-1, keepdims=True)
    acc_sc[...] = alpha * acc_sc[...] + jnp.dot(
        p.astype(BF16), vc_ref[...].reshape(chunk * N_HEADS, HEAD_DIM).astype(BF16), preferred_element_type=F32)
    m_sc[...] = m_new

    @pl.when(c == pl.num_programs(1) - 1)
    def _():
        o_ref[...] = (acc_sc[...] / l_sc[...]).reshape(t_new, N_HEADS, HEAD_DIM)


def _attn_sample(proj3, cache_k, cache_v, layer, batch, t_new):
    cache_len = cache_k.shape[2]
    chunk = 256
    rows_n = t_new * N_HEADS
    new_blk = (t_new, N_HEADS, HEAD_DIM)
    cache_blk = (None, None, chunk, N_HEADS, HEAD_DIM)
    return pl.pallas_call(
        functools.partial(_attn_sample_kernel, cache_len=cache_len, chunk=chunk),
        grid=(batch, cache_len // chunk),
        in_specs=[pl.BlockSpec(new_blk, lambda b, c: (b, 0, 0)),
                  pl.BlockSpec(new_blk, lambda b, c: (b, 1, 0)),
                  pl.BlockSpec(new_blk, lambda b, c: (b, 2, 0)),
                  pl.BlockSpec(cache_blk, lambda b, c: (layer, b, c, 0, 0)),
                  pl.BlockSpec(cache_blk, lambda b, c: (layer, b, c, 0, 0))],
        out_specs=pl.BlockSpec(new_blk, lambda b, c: (b, 0, 0)),
        out_shape=jax.ShapeDtypeStruct((batch * t_new, N_HEADS, HEAD_DIM), F32),
        scratch_shapes=[pltpu.VMEM((rows_n, 1), F32), pltpu.VMEM((rows_n, 1), F32),
                        pltpu.VMEM((rows_n, HEAD_DIM), F32),
                        pltpu.VMEM((cache_len // chunk, rows_n, chunk * N_HEADS), F32)],
        compiler_params=_params("arbitrary", "arbitrary"),
        name="attn_sample",
    )(proj3, proj3, proj3, cache_k, cache_v)


def _conv_kernel(ga_ref, gb_ref, st_ref, wdw_ref, bdw_ref, g_ref, b_ref, o_ref, so_ref, ubuf, cbuf, *, ts, lane_chunk):
    j = pl.program_id(1)
    first = STATE_PAD - CONV_STATE

    @pl.when(j == 0)
    def _():
        ubuf[0:STATE_PAD, :] = st_ref[...]
        ubuf[STATE_PAD + ts:STATE_PAD + ts + SUBLANES, :] = jnp.zeros((SUBLANES, D_CONV), F32)

    gb = gb_ref[...]
    ubuf[STATE_PAD:STATE_PAD + ts, :] = ga_ref[...] * jax.nn.sigmoid(gb)

    for c0 in range(0, D_CONV, lane_chunk):
        lanes = slice(c0, c0 + lane_chunk)
        acc = jnp.broadcast_to(bdw_ref[:, lanes], (ts, lane_chunk))
        for rho in range(SUBLANES):
            part = None
            for base in range(0, STATE_PAD + SUBLANES, SUBLANES):
                w = base + rho - first
                if 0 <= w < CONV_WIDTH:
                    term = ubuf[base:base + ts + SUBLANES, lanes] * wdw_ref[w:w + 1, lanes]
                    part = term if part is None else part + term
            acc = acc + part[rho:rho + ts, :]
        cbuf[:, lanes] = acc

    conv = cbuf[...]
    mu = jnp.mean(conv, axis=-1, keepdims=True)
    xc = conv - mu
    y = xc * lax.rsqrt(jnp.mean(xc * xc, axis=-1, keepdims=True) + NORM_EPS)
    y = y * g_ref[...] + b_ref[...]
    o_ref[...] = (y * jax.nn.sigmoid(y)).astype(o_ref.dtype)

    @pl.when(j == pl.num_programs(1) - 1)
    def _():
        so_ref[...] = ubuf[ts + first:ts + STATE_PAD, :]

    tail = ubuf[ts:ts + STATE_PAD, :]
    ubuf[0:STATE_PAD, :] = tail


def _conv_module(proj, state_pad, w_dw, b_dw, g_ln, b_ln, layer, batch, seq, ts, out_dtype):
    steps = seq // ts
    ga_col = 3 * D_ATTN // D_CONV
    vec = lambda a: a.reshape(a.shape[0], 1, D_CONV)
    vspec = pl.BlockSpec((None, 1, D_CONV), lambda b, j: (layer, 0, 0))
    return pl.pallas_call(
        functools.partial(_conv_kernel, ts=ts, lane_chunk=512),
        grid=(batch, steps),
        in_specs=[pl.BlockSpec((ts, D_CONV), lambda b, j: (b * steps + j, ga_col)),
                  pl.BlockSpec((ts, D_CONV), lambda b, j: (b * steps + j, ga_col + 1)),
                  pl.BlockSpec((None, STATE_PAD, D_CONV), lambda b, j: (b, 0, 0)),
                  pl.BlockSpec((None, CONV_WIDTH, D_CONV), lambda b, j: (layer, 0, 0)),
                  vspec, vspec, vspec],
        out_specs=[pl.BlockSpec((ts, D_CONV), lambda b, j: (b * steps + j, 0)),
                   pl.BlockSpec((None, CONV_STATE, D_CONV), lambda b, j: (b, 0, 0))],
        out_shape=[jax.ShapeDtypeStruct((batch * seq, D_CONV), out_dtype),
                   jax.ShapeDtypeStruct((batch, CONV_STATE, D_CONV), F32)],
        scratch_shapes=[pltpu.VMEM((STATE_PAD + ts + SUBLANES, D_CONV), F32), pltpu.VMEM((ts, D_CONV), F32)],
        compiler_params=_params("parallel", "arbitrary"),
        name="conv_module",
    )(proj, proj, state_pad, w_dw, vec(b_dw), vec(g_ln), vec(b_ln))


CAND_PAD = 1.0e9
TOKEN_UNROLL = 16
HEAD_GROUP = 4


def _cand_plan():
    tiles = [("a", 0, 0), ("a", 0, SUBLANES), ("a", 1, 0), ("b", 0, SUBLANES)] + [("b", b, 0) for b in range(5)]
    flat, seen = [], set()
    for kind, fixed, start in tiles:
        for r in range(SUBLANES):
            a, b = (fixed, start + r) if kind == "a" else (start + r, fixed)
            if (a + 1) * (b + 1) <= PEER_TOPK and (a, b) not in seen:
                seen.add((a, b))
                flat.append(float(a * PEER_TOPK + b))
            else:
                flat.append(CAND_PAD)
    need = {(a, b) for a in range(PEER_TOPK) for b in range(PEER_TOPK) if (a + 1) * (b + 1) <= PEER_TOPK}
    assert seen == need
    return tiles, flat


def _peer_topk_kernel(q_ref, keys_ref, flat_ref, g_ref, s_sc, v_sc, i_sc, cand_sc, cid_sc, top_sc,
                      e_sc, w_sc, i1t_sc, i2t_sc, wt_sc, *, tiles):
    tm = q_ref.shape[0]
    nt = (((1,), (1,)), ((), ()))
    iota_keys = lax.broadcasted_iota(jnp.int32, (N_KEYS, tm), 0).astype(F32)
    flat = flat_ref[...]
    valid = flat < CAND_PAD

    def head_group(hg, carry):
        for hh in range(HEAD_GROUP):
            h = hg * HEAD_GROUP + hh
            col = pl.multiple_of(h * PEER_QDIM, PEER_QDIM)
            for half in range(2):
                qh = q_ref[:, pl.ds(col + half * N_KEYS, N_KEYS)].astype(BF16)
                kh = keys_ref[h, half].astype(BF16)
                s_sc[hh * 2 + half] = lax.dot_general(kh, qh, nt, preferred_element_type=F32)

        def key_step(k, c):
            for ch in range(2 * HEAD_GROUP):
                s = s_sc[ch]
                m = jnp.max(s, axis=0, keepdims=True)
                ix = jnp.min(jnp.where(s == m, iota_keys, float(N_KEYS)), axis=0, keepdims=True)
                v_sc[ch, pl.ds(k, 1), :] = m
                i_sc[ch, pl.ds(k, 1), :] = ix
                s_sc[ch] = jnp.where(iota_keys == ix, NEG_INF, s)
            return c

        lax.fori_loop(0, PEER_TOPK, key_step, 0)

        for hh in range(HEAD_GROUP):
            v1, v2, i1, i2 = v_sc[2 * hh], v_sc[2 * hh + 1], i_sc[2 * hh], i_sc[2 * hh + 1]
            sums, ids = [], []
            for kind, fixed, start in tiles:
                one, rng = slice(fixed, fixed + 1), slice(start, start + SUBLANES)
                ra, rb = (one, rng) if kind == "a" else (rng, one)
                sums.append(v1[ra, :] + v2[rb, :])
                ids.append(i1[ra, :] * float(N_KEYS) + i2[rb, :])
            cand_sc[hh] = jnp.where(valid, jnp.concatenate(sums, axis=0), NEG_INF)
            cid_sc[hh] = jnp.concatenate(ids, axis=0)

        def cand_step(k, c):
            for hh in range(HEAD_GROUP):
                cnd = cand_sc[hh]
                m = jnp.max(cnd, axis=0, keepdims=True)
                pos = jnp.min(jnp.where(cnd == m, flat, 2.0 * CAND_PAD), axis=0, keepdims=True)
                hit = flat == pos
                top_sc[hh, pl.ds(k, 1), :] = m
                e_sc[pl.ds((hg * HEAD_GROUP + hh) * PEER_TOPK + k, 1), :] = jnp.max(
                    jnp.where(hit, cid_sc[hh], -1.0), axis=0, keepdims=True)
                cand_sc[hh] = jnp.where(hit, NEG_INF, cnd)
            return c

        lax.fori_loop(0, PEER_TOPK, cand_step, 0)

        for hh in range(HEAD_GROUP):
            top = top_sc[hh]
            ex = jnp.exp(top - top[0:1, :])
            out_rows = pl.ds(pl.multiple_of((hg * HEAD_GROUP + hh) * PEER_TOPK, PEER_TOPK), PEER_TOPK)
            w_sc[out_rows, :] = ex / jnp.sum(ex, axis=0, keepdims=True)
        return carry

    lax.fori_loop(0, PEER_HEADS // HEAD_GROUP, head_group, 0)

    e = e_sc[...]
    i1 = jnp.floor(e * (1.0 / N_KEYS))
    i1t_sc[...] = jnp.transpose(i1)
    i2t_sc[...] = jnp.transpose(e - i1 * float(N_KEYS))
    wt_sc[...] = jnp.transpose(w_sc[...])

    sub = lax.broadcasted_iota(jnp.int32, (N_KEYS, PEER_HEADS * PEER_TOPK), 0).astype(F32)

    def token(t, carry):
        i1row = i1t_sc[pl.ds(t, 1), :]
        i2row = i2t_sc[pl.ds(t, 1), :]
        wrow = wt_sc[pl.ds(t, 1), :]
        lmat = jnp.where(sub == i1row, wrow, 0.0).astype(BF16)
        rmat = jnp.where(sub == i2row, 1.0, 0.0).astype(BF16)
        g_ref[t] = lax.dot_general(lmat, rmat, nt, preferred_element_type=F32)
        return carry

    lax.fori_loop(0, tm, token, 0, unroll=TOKEN_UNROLL)


def _peer_topk(q, sub_keys, layer):
    t = q.shape[0]
    tm = LANES
    slots = PEER_HEADS * PEER_TOPK
    tiles, flat = _cand_plan()
    ncand = len(flat)
    flat_arr = jnp.broadcast_to(jnp.asarray(flat, F32)[:, None], (ncand, tm))
    sc = lambda *shape: pltpu.VMEM(shape + (tm,), F32)
    return pl.pallas_call(
        functools.partial(_peer_topk_kernel, tiles=tiles),
        grid=(t // tm,),
        in_specs=[pl.BlockSpec((tm, PEER_HEADS * PEER_QDIM), lambda i: (i, 0)),
                  pl.BlockSpec((None, PEER_HEADS, 2, N_KEYS, PEER_QDIM // 2), lambda i: (layer, 0, 0, 0, 0)),
                  pl.BlockSpec((ncand, tm), lambda i: (0, 0))],
        out_specs=pl.BlockSpec((tm, N_KEYS, N_KEYS), lambda i: (i, 0, 0)),
        out_shape=jax.ShapeDtypeStruct((t, N_KEYS, N_KEYS), F32),
        scratch_shapes=[sc(2 * HEAD_GROUP, N_KEYS), sc(2 * HEAD_GROUP, PEER_TOPK), sc(2 * HEAD_GROUP, PEER_TOPK),
                        sc(HEAD_GROUP, ncand), sc(HEAD_GROUP, ncand), sc(HEAD_GROUP, PEER_TOPK),
                        sc(slots), sc(slots),
                        pltpu.VMEM((tm, slots), F32), pltpu.VMEM((tm, slots), F32), pltpu.VMEM((tm, slots), F32)],
        compiler_params=_params("parallel"),
        name="peer_topk",
    )(q, sub_keys, flat_arr)


def _gelu(x):
    return 0.5 * x * (1.0 + lax.erf(x * (2.0 ** -0.5)))


G_ROWS = 8


def _peer_up_kernel(h_ref, u_ref, g_ref, o_ref, *, nsub):
    nt = (((1,), (1,)), ((), ()))
    tm = h_ref.shape[0]
    g2 = g_ref.reshape(tm * G_ROWS, N_KEYS)
    first_row = (pl.program_id(1) % (G_ROWS // nsub)) * nsub
    hb = h_ref[...].astype(BF16)
    half = nsub // 2
    for c0 in range(0, nsub, half):
        rows = slice(c0 * N_KEYS, (c0 + half) * N_KEYS)
        act = _gelu(lax.dot_general(hb, u_ref[rows, :].astype(BF16), nt, preferred_element_type=F32))
        for c in range(half):
            g = g2[pl.ds(first_row + c0 + c, tm, stride=G_ROWS), :]
            lanes = slice((c0 + c) * N_KEYS, (c0 + c + 1) * N_KEYS)
            o_ref[:, lanes] = (g * act[:, c * N_KEYS:(c + 1) * N_KEYS]).astype(o_ref.dtype)


def _peer_up(h, peer_u, g3, layer, tm, tn):
    m = h.shape[0]
    nsub = tn // N_KEYS
    phases = G_ROWS // nsub
    return pl.pallas_call(
        functools.partial(_peer_up_kernel, nsub=nsub),
        grid=(m // tm, N_EXPERTS // tn),
        in_specs=[pl.BlockSpec((tm, D_MODEL), lambda i, j: (i, 0)),
                  pl.BlockSpec((None, tn, D_MODEL), lambda i, j: (layer, j, 0)),
                  pl.BlockSpec((tm, G_ROWS, N_KEYS), lambda i, j: (i, j // phases, 0))],
        out_specs=pl.BlockSpec((tm, tn), lambda i, j: (i, j)),
        out_shape=jax.ShapeDtypeStruct((m, N_EXPERTS), BF16),
        compiler_params=_params("parallel", "arbitrary"),
        name="peer_up",
    )(h, peer_u, g3)


def _peer_down_kernel(w_ref, v_ref, x_ref, gate_ref, o_ref):
    k = pl.program_id(2)

    @pl.when(k == 0)
    def _():
        o_ref[...] = jnp.zeros_like(o_ref)

    o_ref[...] += jnp.dot(w_ref[...], v_ref[...].astype(BF16), preferred_element_type=F32)

    @pl.when(k == pl.num_programs(2) - 1)
    def _():
        o_ref[...] = x_ref[...] + gate_ref[...] * o_ref[...]


def _peer_down(w, peer_v, x, gate, layer, rows_per_group, tm, tn, tk):
    m = w.shape[0]
    tpg = max(rows_per_group // tm, 1)
    r = gate.shape[1]
    if r == 1:
        gate_spec = pl.BlockSpec((None, 1, tn), lambda i, j, k: (i // tpg, 0, j))
    else:
        gate_spec = pl.BlockSpec((None, tm, tn), lambda i, j, k: (0, i, j))
    return pl.pallas_call(
        _peer_down_kernel,
        grid=(m // tm, D_MODEL // tn, N_EXPERTS // tk),
        in_specs=[pl.BlockSpec((tm, tk), lambda i, j, k: (i, k)),
                  pl.BlockSpec((None, tk, tn), lambda i, j, k: (layer, k, j)),
                  pl.BlockSpec((tm, tn), lambda i, j, k: (i, j)),
                  gate_spec],
        out_specs=pl.BlockSpec((tm, tn), lambda i, j, k: (i, j)),
        out_shape=jax.ShapeDtypeStruct((m, D_MODEL), F32),
        compiler_params=_params("parallel", "parallel", "arbitrary"),
        name="peer_down",
    )(w, peer_v, x, gate)


def _layer(x, mods, rows_per_group, tm, attn_fn, conv_fn, layer, weights, act_dtype):
    (g_mix, g_ffn, w_in, w_out, w_peer_q, peer_sub_keys, peer_u, peer_v) = weights
    shift1, scale1, gate1, shift2, scale2, gate2 = mods
    m = x.shape[0]

    h = _norm_mod(x, g_mix[layer], scale1, shift1, rows_per_group, min(tm, 512), act_dtype)
    proj, k_rows, v_rows = _in_proj(h, w_in, layer, tm, 512)
    attn = attn_fn(proj)
    conv, conv_state = conv_fn(proj)
    x = _matmul_concat_res(attn, conv, w_out, layer, tm, 512, x, gate1, rows_per_group)

    h2 = _norm_mod(x, g_ffn[layer], scale2, shift2, rows_per_group, min(tm, 512), act_dtype)
    q = _matmul(h2, w_peer_q, layer, tm, 512)
    pad = (-m) % LANES
    g3 = _peer_topk(jnp.pad(q, ((0, pad), (0, 0))) if pad else q, peer_sub_keys, layer)
    w = _peer_up(h2, peer_u, g3, layer, tm, 512)
    x = _peer_down(w, peer_v, x, gate2, layer, rows_per_group, tm, 1024, 2048)
    return x, k_rows, v_rows, conv_state


def kernel(x_prompt, x_sample, cache_k, cache_v, state_conv, c_prompt, c_sample, w_mod, b_mod, g_mix, g_ffn, w_in,
           w_dw, b_dw, g_conv_ln, b_conv_ln, w_out, w_peer_q, peer_sub_keys, peer_u, peer_v, g_final):
    depth = w_mod.shape[0]
    batch, seq, _ = x_prompt.shape
    dbatch, dseq, _ = x_sample.shape
    mp, ms = batch * seq, dbatch * dseq

    c_all = jnp.concatenate([c_prompt, c_sample], axis=0)
    c_pad = (-c_all.shape[0]) % 8
    mod = _modulation(jnp.pad(c_all, ((0, c_pad), (0, 0))), w_mod, b_mod)

    xp = x_prompt.reshape(mp, D_MODEL)
    xs = x_sample.reshape(ms, D_MODEL)
    weights = (g_mix, g_ffn, w_in, w_out, w_peer_q, peer_sub_keys, peer_u, peer_v)
    zero_state = jnp.zeros((batch, STATE_PAD, D_CONV), F32)
    outs = {name: [] for name in ("kp", "vp", "cp", "ks", "vs", "cs")}

    for l in range(depth):
        mod_p = mod[l, :batch].reshape(batch, N_MOD, 1, D_MODEL)
        mods_p = tuple(mod_p[:, i] for i in range(N_MOD))
        mod_s = mod[l, batch:batch + dbatch].reshape(dbatch, N_MOD, D_MODEL)
        mods_s = tuple(jnp.repeat(mod_s[:, i], dseq, axis=0)[None] for i in range(N_MOD))

        conv_p = functools.partial(_conv_module, state_pad=zero_state, w_dw=w_dw, b_dw=b_dw, g_ln=g_conv_ln,
                                   b_ln=b_conv_ln, layer=l, batch=batch, seq=seq, ts=64, out_dtype=BF16)
        xp, k_p, v_p, cstate_p = _layer(xp, mods_p, seq, 1024, functools.partial(_attn_prompt, batch=batch, seq=seq),
                                        conv_p, l, weights, BF16)

        state_s = jnp.pad(state_conv[l], ((0, 0), (STATE_PAD - CONV_STATE, 0), (0, 0)))
        conv_s = functools.partial(_conv_module, state_pad=state_s, w_dw=w_dw, b_dw=b_dw, g_ln=g_conv_ln,
                                   b_ln=b_conv_ln, layer=l, batch=dbatch, seq=dseq, ts=dseq, out_dtype=F32)

        def attn_s(proj, l=l):
            o = _attn_sample(proj.reshape(ms, D_IN // HEAD_DIM, HEAD_DIM), cache_k, cache_v, l, dbatch, dseq)
            return o.reshape(ms, D_ATTN)

        xs, k_s, v_s, cstate_s = _layer(xs, mods_s, ms, ms, attn_s, conv_s, l, weights, F32)

        keep = min(cache_k.shape[2], seq)
        heads = lambda a, b, s: a.reshape(b, s, N_HEADS, HEAD_DIM)
        outs["kp"].append(heads(k_p, batch, seq)[:, seq - keep:])
        outs["vp"].append(heads(v_p, batch, seq)[:, seq - keep:])
        outs["cp"].append(cstate_p)
        outs["ks"].append(heads(k_s, dbatch, dseq))
        outs["vs"].append(heads(v_s, dbatch, dseq))
        outs["cs"].append(cstate_s)

    y_prompt = _final_norm(xp, g_final, 512).reshape(batch, seq, D_MODEL)
    y_sample = _final_norm(xs, g_final, ms).reshape(dbatch, dseq, D_MODEL)
    return (y_prompt, y_sample, jnp.stack(outs["kp"]), jnp.stack(outs["vp"]), jnp.stack(outs["cp"]),
            jnp.stack(outs["ks"]), jnp.stack(outs["vs"]), jnp.stack(outs["cs"]))
```

```python
import functools

import jax
import jax.numpy as jnp
from jax import lax
from jax.experimental import pallas as pl
from jax.experimental.pallas import tpu as pltpu

F32 = jnp.float32
BF16 = jnp.bfloat16

D_MODEL = 4096
N_HEADS = 16
HEAD_DIM = 128
D_ATTN = N_HEADS * HEAD_DIM
D_CONV = D_MODEL - D_ATTN
CONV_WIDTH = 31
CONV_STATE = CONV_WIDTH - 1
DILATIONS = (1, 4, 16)
ATTN_BLOCK = 128
D_IN = 3 * D_ATTN + 2 * D_CONV
N_MOD = 6
PEER_HEADS = 8
PEER_TOPK = 16
N_KEYS = 128
N_EXPERTS = N_KEYS * N_KEYS
PEER_QDIM = 256
NORM_EPS = 1e-6
ATTN_SCALE = HEAD_DIM ** -0.5

LANES = 128
SUBLANES = 8
STATE_PAD = 32
VMEM_LIMIT = 56 * 1024 * 1024
NEG_INF = float("-inf")


def _params(*sem):
    return pltpu.CompilerParams(dimension_semantics=sem, vmem_limit_bytes=VMEM_LIMIT)


def _mod_kernel(c_ref, w_ref, b_ref, o_ref):
    c = c_ref[...]
    a = (c * jax.nn.sigmoid(c)).astype(BF16)
    o_ref[...] = jnp.dot(a, w_ref[...].astype(BF16), preferred_element_type=F32) + b_ref[...]


def _modulation(c_all, w_mod, b_mod):
    depth, _, n = w_mod.shape
    r = c_all.shape[0]
    tn = 512
    return pl.pallas_call(
        _mod_kernel,
        grid=(depth, n // tn),
        in_specs=[pl.BlockSpec((r, D_MODEL), lambda l, j: (0, 0)),
                  pl.BlockSpec((None, D_MODEL, tn), lambda l, j: (l, 0, j)),
                  pl.BlockSpec((None, 1, tn), lambda l, j: (l, 0, j))],
        out_specs=pl.BlockSpec((None, r, tn), lambda l, j: (l, 0, j)),
        out_shape=jax.ShapeDtypeStruct((depth, r, n), F32),
        compiler_params=_params("parallel", "parallel"),
        name="modulation",
    )(c_all, w_mod, b_mod.reshape(depth, 1, n))


def _norm_mod_kernel(x_ref, g_ref, sc_ref, sh_ref, o_ref):
    x = x_ref[...]
    y = x * lax.rsqrt(jnp.mean(x * x, axis=-1, keepdims=True) + NORM_EPS) * g_ref[...]
    o_ref[...] = (y * (1.0 + sc_ref[...]) + sh_ref[...]).astype(o_ref.dtype)


def _norm_kernel(x_ref, g_ref, o_ref):
    x = x_ref[...]
    y = x * lax.rsqrt(jnp.mean(x * x, axis=-1, keepdims=True) + NORM_EPS) * g_ref[...]
    o_ref[...] = y.astype(o_ref.dtype)


def _row_spec(arr, tm, tn, tiles_per_group):
    r = arr.shape[1]
    rows = 1 if r == 1 else tm
    if r == 1:
        return pl.BlockSpec((None, rows, tn), lambda i, j: (i // tiles_per_group, 0, j))
    return pl.BlockSpec((None, rows, tn), lambda i, j: (0, i, j))


def _norm_mod(x, g, scale, shift, rows_per_group, tm, out_dtype):
    m = x.shape[0]
    tpg = max(rows_per_group // tm, 1)
    g2 = g.reshape(1, D_MODEL)
    return pl.pallas_call(
        _norm_mod_kernel,
        grid=(m // tm, 1),
        in_specs=[pl.BlockSpec((tm, D_MODEL), lambda i, j: (i, 0)),
                  pl.BlockSpec((1, D_MODEL), lambda i, j: (0, 0)),
                  _row_spec(scale, tm, D_MODEL, tpg),
                  _row_spec(shift, tm, D_MODEL, tpg)],
        out_specs=pl.BlockSpec((tm, D_MODEL), lambda i, j: (i, 0)),
        out_shape=jax.ShapeDtypeStruct((m, D_MODEL), out_dtype),
        compiler_params=_params("parallel", "arbitrary"),
        name="norm_mod",
    )(x, g2, scale, shift)


def _final_norm(x, g, tm):
    m = x.shape[0]
    return pl.pallas_call(
        _norm_kernel,
        grid=(m // tm,),
        in_specs=[pl.BlockSpec((tm, D_MODEL), lambda i: (i, 0)),
                  pl.BlockSpec((1, D_MODEL), lambda i: (0, 0))],
        out_specs=pl.BlockSpec((tm, D_MODEL), lambda i: (i, 0)),
        out_shape=jax.ShapeDtypeStruct((m, D_MODEL), F32),
        compiler_params=_params("parallel"),
        name="final_norm",
    )(x, g.reshape(1, D_MODEL))


def _follow_first_tile(i, j, last):
    return jnp.where(i == 0, j, last)


def _mm_kernel(a_ref, as_ref, w_ref, o_ref, os_ref):
    w = w_ref[...].astype(BF16)
    o_ref[...] = jnp.dot(a_ref[...].astype(BF16), w, preferred_element_type=F32)

    @pl.when(pl.program_id(0) == 0)
    def _():
        os_ref[...] = jnp.dot(as_ref[...].astype(BF16), w, preferred_element_type=F32)


def _matmul(a, a_s, w, layer, tm, tn):
    m, k = a.shape
    ms = a_s.shape[0]
    n = w.shape[2]
    last = n // tn - 1
    return pl.pallas_call(
        _mm_kernel,
        grid=(m // tm, n // tn),
        in_specs=[pl.BlockSpec((tm, k), lambda i, j: (i, 0)),
                  pl.BlockSpec((ms, k), lambda i, j: (0, 0)),
                  pl.BlockSpec((None, k, tn), lambda i, j: (layer, 0, j))],
        out_specs=[pl.BlockSpec((tm, tn), lambda i, j: (i, j)),
                   pl.BlockSpec((ms, tn), lambda i, j: (0, _follow_first_tile(i, j, last)))],
        out_shape=[jax.ShapeDtypeStruct((m, n), F32), jax.ShapeDtypeStruct((ms, n), F32)],
        compiler_params=_params("arbitrary", "arbitrary"),
        name="matmul",
    )(a, a_s, w)


def _in_proj_kernel(a_ref, as_ref, w_ref, kin_ref, vin_ref, o_ref, k_ref, v_ref, os_ref, ks_ref, vs_ref, *, kv_tiles):
    del kin_ref, vin_ref
    j = pl.program_id(1)
    is_k = (j >= kv_tiles) & (j < 2 * kv_tiles)
    is_v = (j >= 2 * kv_tiles) & (j < 3 * kv_tiles)
    w = w_ref[...].astype(BF16)
    acc = jnp.dot(a_ref[...].astype(BF16), w, preferred_element_type=F32)
    o_ref[...] = acc

    @pl.when(is_k)
    def _():
        k_ref[...] = acc

    @pl.when(is_v)
    def _():
        v_ref[...] = acc

    @pl.when(pl.program_id(0) == 0)
    def _():
        acc_s = jnp.dot(as_ref[...].astype(BF16), w, preferred_element_type=F32)
        os_ref[...] = acc_s

        @pl.when(is_k)
        def _():
            ks_ref[...] = acc_s

        @pl.when(is_v)
        def _():
            vs_ref[...] = acc_s


def _in_proj(a, a_s, w, k_all, v_all, layer, tm, tn):
    m, k = a.shape
    ms = a_s.shape[0]
    n = w.shape[2]
    last = n // tn - 1
    kv_tiles = D_ATTN // tn
    kv_col = lambda j, first: jnp.clip(j - first, 0, kv_tiles - 1)

    def kv_spec(first):
        return pl.BlockSpec((None, tm, tn), lambda i, j: (layer, i, kv_col(j, first)))

    def kv_s_spec(first):
        return pl.BlockSpec((ms, tn), lambda i, j: (0, kv_col(_follow_first_tile(i, j, last), first)))

    return pl.pallas_call(
        functools.partial(_in_proj_kernel, kv_tiles=kv_tiles),
        grid=(m // tm, n // tn),
        in_specs=[pl.BlockSpec((tm, k), lambda i, j: (i, 0)),
                  pl.BlockSpec((ms, k), lambda i, j: (0, 0)),
                  pl.BlockSpec((None, k, tn), lambda i, j: (layer, 0, j)),
                  pl.BlockSpec(memory_space=pl.ANY),
                  pl.BlockSpec(memory_space=pl.ANY)],
        out_specs=[pl.BlockSpec((tm, tn), lambda i, j: (i, j)), kv_spec(kv_tiles), kv_spec(2 * kv_tiles),
                   pl.BlockSpec((ms, tn), lambda i, j: (0, _follow_first_tile(i, j, last))),
                   kv_s_spec(kv_tiles), kv_s_spec(2 * kv_tiles)],
        out_shape=[jax.ShapeDtypeStruct((m, n), F32), jax.ShapeDtypeStruct(k_all.shape, F32),
                   jax.ShapeDtypeStruct(v_all.shape, F32), jax.ShapeDtypeStruct((ms, n), F32),
                   jax.ShapeDtypeStruct((ms, D_ATTN), F32), jax.ShapeDtypeStruct((ms, D_ATTN), F32)],
        input_output_aliases={3: 1, 4: 2},
        compiler_params=_params("arbitrary", "arbitrary"),
        name="in_proj",
    )(a, a_s, w, k_all, v_all)


def _mm2_res_kernel(a1_ref, a2_ref, a1s_ref, a2s_ref, w1_ref, w2_ref, x_ref, gate_ref, xs_ref, gates_ref,
                    o_ref, os_ref):
    w1 = w1_ref[...].astype(BF16)
    w2 = w2_ref[...].astype(BF16)
    acc = jnp.dot(a1_ref[...].astype(BF16), w1, preferred_element_type=F32)
    acc = acc + jnp.dot(a2_ref[...].astype(BF16), w2, preferred_element_type=F32)
    o_ref[...] = x_ref[...] + gate_ref[...] * acc

    @pl.when(pl.program_id(0) == 0)
    def _():
        acc_s = jnp.dot(a1s_ref[...].astype(BF16), w1, preferred_element_type=F32)
        acc_s = acc_s + jnp.dot(a2s_ref[...].astype(BF16), w2, preferred_element_type=F32)
        os_ref[...] = xs_ref[...] + gates_ref[...] * acc_s


def _matmul_concat_res(a1, a2, a1_s, a2_s, w, layer, tm, tn, x, gate, x_s, gate_s, rows_per_group):
    m, k1 = a1.shape
    k2 = a2.shape[1]
    ms = a1_s.shape[0]
    assert k1 == k2 and w.shape[1] == k1 + k2
    n = w.shape[2]
    last = n // tn - 1
    tpg = max(rows_per_group // tm, 1)
    s_col = lambda i, j: (0, _follow_first_tile(i, j, last))
    return pl.pallas_call(
        _mm2_res_kernel,
        grid=(m // tm, n // tn),
        in_specs=[pl.BlockSpec((tm, k1), lambda i, j: (i, 0)),
                  pl.BlockSpec((tm, k2), lambda i, j: (i, 0)),
                  pl.BlockSpec((ms, k1), lambda i, j: (0, 0)),
                  pl.BlockSpec((ms, k2), lambda i, j: (0, 0)),
                  pl.BlockSpec((None, k1, tn), lambda i, j: (layer, 0, j)),
                  pl.BlockSpec((None, k2, tn), lambda i, j: (layer, 1, j)),
                  pl.BlockSpec((tm, tn), lambda i, j: (i, j)),
                  _row_spec(gate, tm, tn, tpg),
                  pl.BlockSpec((ms, tn), s_col),
                  pl.BlockSpec((None, ms, tn), lambda i, j: (0,) + s_col(i, j))],
        out_specs=[pl.BlockSpec((tm, tn), lambda i, j: (i, j)), pl.BlockSpec((ms, tn), s_col)],
        out_shape=[jax.ShapeDtypeStruct((m, n), F32), jax.ShapeDtypeStruct((ms, n), F32)],
        compiler_params=_params("arbitrary", "arbitrary"),
        name="matmul_concat",
    )(a1, a2, a1_s, a2_s, w, w, x, gate, x_s, gate_s)


ATTN_GROUP = 4


def _attn_prompt_kernel(q_ref, k_ref, v_ref, o_ref, o_sc, lse_sc):
    seq = q_ref.shape[0]
    nt = (((1,), (1,)), ((), ()))
    row = lax.broadcasted_iota(jnp.int32, (ATTN_BLOCK, 2 * ATTN_BLOCK), 0)
    col = lax.broadcasted_iota(jnp.int32, (ATTN_BLOCK, 2 * ATTN_BLOCK), 1)
    band_mask = (col >= row) & (col <= row + ATTN_BLOCK)
    causal_mask = (lax.broadcasted_iota(jnp.int32, (ATTN_BLOCK, ATTN_BLOCK), 1)
                   <= lax.broadcasted_iota(jnp.int32, (ATTN_BLOCK, ATTN_BLOCK), 0))

    def rows_of(start, n, dil):
        return pl.ds(start, n, stride=dil) if dil > 1 else pl.ds(start, n)

    def run_group(p, dil, units):
        qs, ks, vs, masks = [], [], [], []
        for start, has_prev in units:
            nk = 2 * ATTN_BLOCK if has_prev else ATTN_BLOCK
            kstart = start - ATTN_BLOCK * dil if has_prev else start
            qs.append(q_ref[rows_of(start, ATTN_BLOCK, dil), :].astype(BF16))
            ks.append(k_ref[rows_of(kstart, nk, dil), :].astype(BF16))
            v = v_ref[rows_of(kstart, nk, dil), :].astype(BF16)
            vs.append(jnp.concatenate([v, jnp.ones((nk, HEAD_DIM), BF16)], axis=-1))
            masks.append(band_mask if has_prev else causal_mask)
        ss = [jnp.where(mk, lax.dot_general(q, k, nt, preferred_element_type=F32) * ATTN_SCALE, NEG_INF)
              for q, k, mk in zip(qs, ks, masks)]
        ms = [jnp.max(s, axis=-1, keepdims=True) for s in ss]
        ps = [jnp.exp(s - m).astype(BF16) for s, m in zip(ss, ms)]
        rs = [jnp.dot(pr, v, preferred_element_type=F32) for pr, v in zip(ps, vs)]
        for (start, _), r, m in zip(units, rs, ms):
            acc, den = r[:, :HEAD_DIM], r[:, HEAD_DIM:]
            rows = rows_of(start, ATTN_BLOCK, dil)
            o_sc[p, rows, :] = acc / den
            lse_sc[p, rows, :] = m + jnp.log(den)

    for p, dil in enumerate(DILATIONS):
        nblk = seq // dil // ATTN_BLOCK
        units = [(r + blk * ATTN_BLOCK * dil, blk > 0) for blk in range(nblk) for r in range(dil)]
        for g0 in range(0, len(units), ATTN_GROUP):
            run_group(p, dil, units[g0:g0 + ATTN_GROUP])

    l0, l1, l2 = lse_sc[0], lse_sc[1], lse_sc[2]
    mx = jnp.maximum(jnp.maximum(l0, l1), l2)
    w0, w1, w2 = jnp.exp(l0 - mx), jnp.exp(l1 - mx), jnp.exp(l2 - mx)
    out = (w0 * o_sc[0] + w1 * o_sc[1] + w2 * o_sc[2]) / (w0 + w1 + w2)
    o_ref[...] = out.astype(o_ref.dtype)


def _attn_prompt(proj, batch, seq):
    blk = (seq, HEAD_DIM)
    return pl.pallas_call(
        _attn_prompt_kernel,
        grid=(batch, N_HEADS),
        in_specs=[pl.BlockSpec(blk, lambda b, h: (b, h)),
                  pl.BlockSpec(blk, lambda b, h: (b, N_HEADS + h)),
                  pl.BlockSpec(blk, lambda b, h: (b, 2 * N_HEADS + h))],
        out_specs=pl.BlockSpec(blk, lambda b, h: (b, h)),
        out_shape=jax.ShapeDtypeStruct((batch * seq, D_ATTN), BF16),
        scratch_shapes=[pltpu.VMEM((len(DILATIONS), seq, HEAD_DIM), F32),
                        pltpu.VMEM((len(DILATIONS), seq, HEAD_DIM), F32)],
        compiler_params=_params("parallel", "parallel"),
        name="attn_prompt",
    )(proj, proj, proj)


def _pattern_count(dist):
    cnt = jnp.zeros(dist.shape, F32)
    for dil in DILATIONS:
        ok = (dist >= 0) & ((dist & (dil - 1)) == 0) & (dist <= ATTN_BLOCK * dil)
        cnt = cnt + jnp.where(ok, 1.0, 0.0)
    return cnt


def _attn_sample_kernel(q_ref, kn_ref, vn_ref, kc_ref, vc_ref, o_ref, m_sc, l_sc, acc_sc, cnt_sc, *, cache_len, chunk):
    b = pl.program_id(0)
    c = pl.program_id(1)
    t_new = q_ref.shape[0]
    rows_n = t_new * N_HEADS
    nt = (((1,), (1,)), ((), ()))
    q = q_ref[...].reshape(rows_n, HEAD_DIM).astype(BF16)

    def pattern_weights(n, key_pos):
        ri = lax.broadcasted_iota(jnp.int32, (rows_n, n), 0)
        ci = lax.broadcasted_iota(jnp.int32, (rows_n, n), 1)
        same_head = (ri & (N_HEADS - 1)) == (ci & (N_HEADS - 1))
        dist = cache_len + (ri >> 4) - (key_pos + (ci >> 4))
        return jnp.where(same_head, _pattern_count(dist), 0.0)

    def scores(k2d, cnt):
        s = lax.dot_general(q, k2d.astype(BF16), nt, preferred_element_type=F32) * ATTN_SCALE
        return jnp.where(cnt > 0.0, s, NEG_INF)

    @pl.when(b == 0)
    def _():
        cnt_sc[c] = pattern_weights(chunk * N_HEADS, c * chunk)

    @pl.when(c == 0)
    def _():
        cnt = pattern_weights(rows_n, cache_len)
        s = scores(kn_ref[...].reshape(rows_n, HEAD_DIM), cnt)
        m = jnp.max(s, axis=-1, keepdims=True)
        p = cnt * jnp.exp(s - m)
        m_sc[...] = m
        l_sc[...] = jnp.sum(p, axis=-1, keepdims=True)
        acc_sc[...] = jnp.dot(p.astype(BF16), vn_ref[...].reshape(rows_n, HEAD_DIM).astype(BF16),
                              preferred_element_type=F32)

    cnt = cnt_sc[c]
    s = scores(kc_ref[...].reshape(chunk * N_HEADS, HEAD_DIM), cnt)
    m_old = m_sc[...]
    m_new = jnp.maximum(m_old, jnp.max(s, axis=-1, keepdims=True))
    alpha = jnp.exp(m_old - m_new)
    p = cnt * jnp.exp(s - m_new)
    l_sc[...] = alpha * l_sc[...] + jnp.sum(p, axis=-1, keepdims=True)
    acc_sc[...] = alpha * acc_sc[...] + jnp.dot(
        p.astype(BF16), vc_ref[...].reshape(chunk * N_HEADS, HEAD_DIM).astype(BF16), preferred_element_type=F32)
    m_sc[...] = m_new

    @pl.when(c == pl.num_programs(1) - 1)
    def _():
        o_ref[...] = (acc_sc[...] / l_sc[...]).reshape(t_new, N_HEADS, HEAD_DIM)


def _attn_sample(proj3, cache_k, cache_v, layer, batch, t_new):
    cache_len = cache_k.shape[2]
    chunk = 256
    rows_n = t_new * N_HEADS
    new_blk = (t_new, N_HEADS, HEAD_DIM)
    cache_blk = (None, None, chunk, N_HEADS, HEAD_DIM)
    return pl.pallas_call(
        functools.partial(_attn_sample_kernel, cache_len=cache_len, chunk=chunk),
        grid=(batch, cache_len // chunk),
        in_specs=[pl.BlockSpec(new_blk, lambda b, c: (b, 0, 0)),
                  pl.BlockSpec(new_blk, lambda b, c: (b, 1, 0)),
                  pl.BlockSpec(new_blk, lambda b, c: (b, 2, 0)),
                  pl.BlockSpec(cache_blk, lambda b, c: (layer, b, c, 0, 0)),
                  pl.BlockSpec(cache_blk, lambda b, c: (layer, b, c, 0, 0))],
        out_specs=pl.BlockSpec(new_blk, lambda b, c: (b, 0, 0)),
        out_shape=jax.ShapeDtypeStruct((batch * t_new, N_HEADS, HEAD_DIM), F32),
        scratch_shapes=[pltpu.VMEM((rows_n, 1), F32), pltpu.VMEM((rows_n, 1), F32),
                        pltpu.VMEM((rows_n, HEAD_DIM), F32),
                        pltpu.VMEM((cache_len // chunk, rows_n, chunk * N_HEADS), F32)],
        compiler_params=_params("arbitrary", "arbitrary"),
        name="attn_sample",
    )(proj3, proj3, proj3, cache_k, cache_v)


def _conv_kernel(ga_ref, gb_ref, st_ref, wdw_ref, bdw_ref, g_ref, b_ref, o_ref, so_ref, ubuf, cbuf, *, ts, lane_chunk):
    j = pl.program_id(1)
    first = STATE_PAD - CONV_STATE

    @pl.when(j == 0)
    def _():
        ubuf[0:STATE_PAD, :] = st_ref[...]
        ubuf[STATE_PAD + ts:STATE_PAD + ts + SUBLANES, :] = jnp.zeros((SUBLANES, D_CONV), F32)

    gb = gb_ref[...]
    ubuf[STATE_PAD:STATE_PAD + ts, :] = ga_ref[...] * jax.nn.sigmoid(gb)

    for c0 in range(0, D_CONV, lane_chunk):
        lanes = slice(c0, c0 + lane_chunk)
        acc = jnp.broadcast_to(bdw_ref[:, lanes], (ts, lane_chunk))
        for rho in range(SUBLANES):
            part = None
            for base in range(0, STATE_PAD + SUBLANES, SUBLANES):
                w = base + rho - first
                if 0 <= w < CONV_WIDTH:
                    term = ubuf[base:base + ts + SUBLANES, lanes] * wdw_ref[w:w + 1, lanes]
                    part = term if part is None else part + term
            acc = acc + part[rho:rho + ts, :]
        cbuf[:, lanes] = acc

    conv = cbuf[...]
    mu = jnp.mean(conv, axis=-1, keepdims=True)
    xc = conv - mu
    y = xc * lax.rsqrt(jnp.mean(xc * xc, axis=-1, keepdims=True) + NORM_EPS)
    y = y * g_ref[...] + b_ref[...]
    o_ref[...] = (y * jax.nn.sigmoid(y)).astype(o_ref.dtype)

    @pl.when(j == pl.num_programs(1) - 1)
    def _():
        so_ref[...] = ubuf[ts + first:ts + STATE_PAD, :]

    tail = ubuf[ts:ts + STATE_PAD, :]
    ubuf[0:STATE_PAD, :] = tail


def _conv_module(proj, state_pad, w_dw, b_dw, g_ln, b_ln, layer, batch, seq, ts, out_dtype):
    steps = seq // ts
    ga_col = 3 * D_ATTN // D_CONV
    vec = lambda a: a.reshape(a.shape[0], 1, D_CONV)
    vspec = pl.BlockSpec((None, 1, D_CONV), lambda b, j: (layer, 0, 0))
    return pl.pallas_call(
        functools.partial(_conv_kernel, ts=ts, lane_chunk=512),
        grid=(batch, steps),
        in_specs=[pl.BlockSpec((ts, D_CONV), lambda b, j: (b * steps + j, ga_col)),
                  pl.BlockSpec((ts, D_CONV), lambda b, j: (b * steps + j, ga_col + 1)),
                  pl.BlockSpec((None, STATE_PAD, D_CONV), lambda b, j: (b, 0, 0)),
                  pl.BlockSpec((None, CONV_WIDTH, D_CONV), lambda b, j: (layer, 0, 0)),
                  vspec, vspec, vspec],
        out_specs=[pl.BlockSpec((ts, D_CONV), lambda b, j: (b * steps + j, 0)),
                   pl.BlockSpec((None, CONV_STATE, D_CONV), lambda b, j: (b, 0, 0))],
        out_shape=[jax.ShapeDtypeStruct((batch * seq, D_CONV), out_dtype),
                   jax.ShapeDtypeStruct((batch, CONV_STATE, D_CONV), F32)],
        scratch_shapes=[pltpu.VMEM((STATE_PAD + ts + SUBLANES, D_CONV), F32), pltpu.VMEM((ts, D_CONV), F32)],
        compiler_params=_params("parallel", "arbitrary"),
        name="conv_module",
    )(proj, proj, state_pad, w_dw, vec(b_dw), vec(g_ln), vec(b_ln))


CAND_PAD = 1.0e9
TOKEN_UNROLL = 16
HEAD_GROUP = 4


def _cand_plan():
    tiles = [("a", 0, 0), ("a", 0, SUBLANES), ("a", 1, 0), ("b", 0, SUBLANES)] + [("b", b, 0) for b in range(5)]
    flat, seen = [], set()
    for kind, fixed, start in tiles:
        for r in range(SUBLANES):
            a, b = (fixed, start + r) if kind == "a" else (start + r, fixed)
            if (a + 1) * (b + 1) <= PEER_TOPK and (a, b) not in seen:
                seen.add((a, b))
                flat.append(float(a * PEER_TOPK + b))
            else:
                flat.append(CAND_PAD)
    need = {(a, b) for a in range(PEER_TOPK) for b in range(PEER_TOPK) if (a + 1) * (b + 1) <= PEER_TOPK}
    assert seen == need
    return tiles, flat


def _peer_topk_kernel(q_ref, keys_ref, flat_ref, g_ref, s_sc, v_sc, i_sc, cand_sc, cid_sc, top_sc,
                      e_sc, w_sc, i1t_sc, i2t_sc, wt_sc, *, tiles):
    tm = q_ref.shape[0]
    nt = (((1,), (1,)), ((), ()))
    iota_keys = lax.broadcasted_iota(jnp.int32, (N_KEYS, tm), 0).astype(F32)
    flat = flat_ref[...]
    valid = flat < CAND_PAD

    def head_group(hg, carry):
        for hh in range(HEAD_GROUP):
            h = hg * HEAD_GROUP + hh
            col = pl.multiple_of(h * PEER_QDIM, PEER_QDIM)
            for half in range(2):
                qh = q_ref[:, pl.ds(col + half * N_KEYS, N_KEYS)].astype(BF16)
                kh = keys_ref[h, half].astype(BF16)
                s_sc[hh * 2 + half] = lax.dot_general(kh, qh, nt, preferred_element_type=F32)

        def key_step(k, c):
            for ch in range(2 * HEAD_GROUP):
                s = s_sc[ch]
                m = jnp.max(s, axis=0, keepdims=True)
                ix = jnp.min(jnp.where(s == m, iota_keys, float(N_KEYS)), axis=0, keepdims=True)
                v_sc[ch, pl.ds(k, 1), :] = m
                i_sc[ch, pl.ds(k, 1), :] = ix
                s_sc[ch] = jnp.where(iota_keys == ix, NEG_INF, s)
            return c

        lax.fori_loop(0, PEER_TOPK, key_step, 0)

        for hh in range(HEAD_GROUP):
            v1, v2, i1, i2 = v_sc[2 * hh], v_sc[2 * hh + 1], i_sc[2 * hh], i_sc[2 * hh + 1]
            sums, ids = [], []
            for kind, fixed, start in tiles:
                one, rng = slice(fixed, fixed + 1), slice(start, start + SUBLANES)
                ra, rb = (one, rng) if kind == "a" else (rng, one)
                sums.append(v1[ra, :] + v2[rb, :])
                ids.append(i1[ra, :] * float(N_KEYS) + i2[rb, :])
            cand_sc[hh] = jnp.where(valid, jnp.concatenate(sums, axis=0), NEG_INF)
            cid_sc[hh] = jnp.concatenate(ids, axis=0)

        def cand_step(k, c):
            for hh in range(HEAD_GROUP):
                cnd = cand_sc[hh]
                m = jnp.max(cnd, axis=0, keepdims=True)
                pos = jnp.min(jnp.where(cnd == m, flat, 2.0 * CAND_PAD), axis=0, keepdims=True)
                hit = flat == pos
                top_sc[hh, pl.ds(k, 1), :] = m
                e_sc[pl.ds((hg * HEAD_GROUP + hh) * PEER_TOPK + k, 1), :] = jnp.max(
                    jnp.where(hit, cid_sc[hh], -1.0), axis=0, keepdims=True)
                cand_sc[hh] = jnp.where(hit, NEG_INF, cnd)
            return c

        lax.fori_loop(0, PEER_TOPK, cand_step, 0)

        for hh in range(HEAD_GROUP):
            top = top_sc[hh]
            ex = jnp.exp(top - top[0:1, :])
            out_rows = pl.ds(pl.multiple_of((hg * HEAD_GROUP + hh) * PEER_TOPK, PEER_TOPK), PEER_TOPK)
            w_sc[out_rows, :] = ex / jnp.sum(ex, axis=0, keepdims=True)
        return carry

    lax.fori_loop(0, PEER_HEADS // HEAD_GROUP, head_group, 0)

    e = e_sc[...]
    i1 = jnp.floor(e * (1.0 / N_KEYS))
    i1t_sc[...] = jnp.transpose(i1)
    i2t_sc[...] = jnp.transpose(e - i1 * float(N_KEYS))
    wt_sc[...] = jnp.transpose(w_sc[...])

    sub = lax.broadcasted_iota(jnp.int32, (N_KEYS, PEER_HEADS * PEER_TOPK), 0).astype(F32)

    def token(t, carry):
        i1row = i1t_sc[pl.ds(t, 1), :]
        i2row = i2t_sc[pl.ds(t, 1), :]
        wrow = wt_sc[pl.ds(t, 1), :]
        lmat = jnp.where(sub == i1row, wrow, 0.0).astype(BF16)
        rmat = jnp.where(sub == i2row, 1.0, 0.0).astype(BF16)
        g_ref[t] = lax.dot_general(lmat, rmat, nt, preferred_element_type=F32)
        return carry

    lax.fori_loop(0, tm, token, 0, unroll=TOKEN_UNROLL)


def _peer_topk(q, sub_keys, layer):
    t = q.shape[0]
    tm = LANES
    slots = PEER_HEADS * PEER_TOPK
    tiles, flat = _cand_plan()
    ncand = len(flat)
    flat_arr = jnp.broadcast_to(jnp.asarray(flat, F32)[:, None], (ncand, tm))
    sc = lambda *shape: pltpu.VMEM(shape + (tm,), F32)
    return pl.pallas_call(
        functools.partial(_peer_topk_kernel, tiles=tiles),
        grid=(t // tm,),
        in_specs=[pl.BlockSpec((tm, PEER_HEADS * PEER_QDIM), lambda i: (i, 0)),
                  pl.BlockSpec((None, PEER_HEADS, 2, N_KEYS, PEER_QDIM // 2), lambda i: (layer, 0, 0, 0, 0)),
                  pl.BlockSpec((ncand, tm), lambda i: (0, 0))],
        out_specs=pl.BlockSpec((tm, N_KEYS, N_KEYS), lambda i: (i, 0, 0)),
        out_shape=jax.ShapeDtypeStruct((t, N_KEYS, N_KEYS), F32),
        scratch_shapes=[sc(2 * HEAD_GROUP, N_KEYS), sc(2 * HEAD_GROUP, PEER_TOPK), sc(2 * HEAD_GROUP, PEER_TOPK),
                        sc(HEAD_GROUP, ncand), sc(HEAD_GROUP, ncand), sc(HEAD_GROUP, PEER_TOPK),
                        sc(slots), sc(slots),
                        pltpu.VMEM((tm, slots), F32), pltpu.VMEM((tm, slots), F32), pltpu.VMEM((tm, slots), F32)],
        compiler_params=_params("parallel"),
        name="peer_topk",
    )(q, sub_keys, flat_arr)


def _gelu(x):
    return 0.5 * x * (1.0 + lax.erf(x * (2.0 ** -0.5)))


G_ROWS = 8


def _gated_gelu_tile(h_bf16, u_rows, g2, first_row, n_rows, n_sub):
    nt = (((1,), (1,)), ((), ()))
    act = _gelu(lax.dot_general(h_bf16, u_rows.astype(BF16), nt, preferred_element_type=F32))
    return [(g2[pl.ds(first_row + c, n_rows, stride=G_ROWS), :] * act[:, c * N_KEYS:(c + 1) * N_KEYS]).astype(BF16)
            for c in range(n_sub)]


def _peer_up_kernel(h_ref, hs_ref, u_ref, g_ref, gs_ref, o_ref, os_ref, *, nsub):
    tm, ms = h_ref.shape[0], hs_ref.shape[0]
    g2 = g_ref.reshape(tm * G_ROWS, N_KEYS)
    gs2 = gs_ref.reshape(ms * G_ROWS, N_KEYS)
    first_row = (pl.program_id(1) % (G_ROWS // nsub)) * nsub
    hb = h_ref[...].astype(BF16)
    half = nsub // 2
    for c0 in range(0, nsub, half):
        pieces = _gated_gelu_tile(hb, u_ref[c0 * N_KEYS:(c0 + half) * N_KEYS, :], g2, first_row + c0, tm, half)
        for c, piece in enumerate(pieces):
            o_ref[:, (c0 + c) * N_KEYS:(c0 + c + 1) * N_KEYS] = piece

    @pl.when(pl.program_id(0) == 0)
    def _():
        pieces = _gated_gelu_tile(hs_ref[...].astype(BF16), u_ref[...], gs2, first_row, ms, nsub)
        for c, piece in enumerate(pieces):
            os_ref[:, c * N_KEYS:(c + 1) * N_KEYS] = piece


def _peer_up(h, h_s, peer_u, g3, g3_s, layer, tm, tn):
    m, ms = h.shape[0], h_s.shape[0]
    nsub = tn // N_KEYS
    phases = G_ROWS // nsub
    last = N_EXPERTS // tn - 1
    return pl.pallas_call(
        functools.partial(_peer_up_kernel, nsub=nsub),
        grid=(m // tm, N_EXPERTS // tn),
        in_specs=[pl.BlockSpec((tm, D_MODEL), lambda i, j: (i, 0)),
                  pl.BlockSpec((ms, D_MODEL), lambda i, j: (0, 0)),
                  pl.BlockSpec((None, tn, D_MODEL), lambda i, j: (layer, j, 0)),
                  pl.BlockSpec((tm, G_ROWS, N_KEYS), lambda i, j: (i, j // phases, 0)),
                  pl.BlockSpec((ms, G_ROWS, N_KEYS), lambda i, j: (0, _follow_first_tile(i, j, last) // phases, 0))],
        out_specs=[pl.BlockSpec((tm, tn), lambda i, j: (i, j)),
                   pl.BlockSpec((ms, tn), lambda i, j: (0, _follow_first_tile(i, j, last)))],
        out_shape=[jax.ShapeDtypeStruct((m, N_EXPERTS), BF16), jax.ShapeDtypeStruct((ms, N_EXPERTS), BF16)],
        compiler_params=_params("arbitrary", "arbitrary"),
        name="peer_up",
    )(h, h_s, peer_u, g3, g3_s)


def _peer_down_kernel(w_ref, ws_ref, v_ref, x_ref, gate_ref, xs_ref, gates_ref, o_ref, os_ref):
    k = pl.program_id(2)
    first, last = k == 0, k == pl.num_programs(2) - 1
    v = v_ref[...].astype(BF16)

    def accumulate(lhs_ref, res_ref, gate, acc_ref):
        @pl.when(first)
        def _():
            acc_ref[...] = jnp.zeros_like(acc_ref)

        acc_ref[...] += jnp.dot(lhs_ref[...], v, preferred_element_type=F32)

        @pl.when(last)
        def _():
            acc_ref[...] = res_ref[...] + gate[...] * acc_ref[...]

    accumulate(w_ref, x_ref, gate_ref, o_ref)

    @pl.when(pl.program_id(0) == 0)
    def _():
        accumulate(ws_ref, xs_ref, gates_ref, os_ref)


def _peer_down(w, w_s, peer_v, x, gate, x_s, gate_s, layer, rows_per_group, tm, tn, tk):
    m, ms = w.shape[0], w_s.shape[0]
    tpg = max(rows_per_group // tm, 1)
    last_j, last_k = D_MODEL // tn - 1, N_EXPERTS // tk - 1
    s_col = lambda i, j, k: (0, _follow_first_tile(i, j, last_j))
    return pl.pallas_call(
        _peer_down_kernel,
        grid=(m // tm, D_MODEL // tn, N_EXPERTS // tk),
        in_specs=[pl.BlockSpec((tm, tk), lambda i, j, k: (i, k)),
                  pl.BlockSpec((ms, tk), lambda i, j, k: (0, _follow_first_tile(i, k, last_k))),
                  pl.BlockSpec((None, tk, tn), lambda i, j, k: (layer, k, j)),
                  pl.BlockSpec((tm, tn), lambda i, j, k: (i, j)),
                  pl.BlockSpec((None, 1, tn), lambda i, j, k: (i // tpg, 0, j)),
                  pl.BlockSpec((ms, tn), s_col),
                  pl.BlockSpec((None, ms, tn), lambda i, j, k: (0,) + s_col(i, j, k))],
        out_specs=[pl.BlockSpec((tm, tn), lambda i, j, k: (i, j)), pl.BlockSpec((ms, tn), s_col)],
        out_shape=[jax.ShapeDtypeStruct((m, D_MODEL), F32), jax.ShapeDtypeStruct((ms, D_MODEL), F32)],
        compiler_params=_params("arbitrary", "arbitrary", "arbitrary"),
        name="peer_down",
    )(w, w_s, peer_v, x, gate, x_s, gate_s)


TM = 1024
TN = 512
TM_NORM = 512


def _layer(xp, xs, mods_p, mods_s, seq, attn_p_fn, attn_s_fn, conv_p_fn, conv_s_fn, k_all, v_all, layer, weights):
    (g_mix, g_ffn, w_in, w_out, w_peer_q, peer_sub_keys, peer_u, peer_v) = weights
    shift1, scale1, gate1, shift2, scale2, gate2 = mods_p
    shift1s, scale1s, gate1s, shift2s, scale2s, gate2s = mods_s
    ms = xs.shape[0]

    hp = _norm_mod(xp, g_mix[layer], scale1, shift1, seq, TM_NORM, BF16)
    hs = _norm_mod(xs, g_mix[layer], scale1s, shift1s, ms, ms, F32)
    proj_p, k_all, v_all, proj_s, k_s, v_s = _in_proj(hp, hs, w_in, k_all, v_all, layer, TM, TN)
    conv_p, cstate_p = conv_p_fn(proj_p)
    conv_s, cstate_s = conv_s_fn(proj_s)
    xp, xs = _matmul_concat_res(attn_p_fn(proj_p), conv_p, attn_s_fn(proj_s), conv_s, w_out, layer, TM, TN,
                                xp, gate1, xs, gate1s, seq)

    h2p = _norm_mod(xp, g_ffn[layer], scale2, shift2, seq, TM_NORM, BF16)
    h2s = _norm_mod(xs, g_ffn[layer], scale2s, shift2s, ms, ms, F32)
    qp, qs = _matmul(h2p, h2s, w_peer_q, layer, TM, TN)
    g3p = _peer_topk(qp, peer_sub_keys, layer)
    g3s = _peer_topk(jnp.pad(qs, ((0, (-ms) % LANES), (0, 0))), peer_sub_keys, layer)
    wp, ws = _peer_up(h2p, h2s, peer_u, g3p, g3s, layer, TM, TN)
    xp, xs = _peer_down(wp, ws, peer_v, xp, gate2, xs, gate2s, layer, seq, TM, 1024, 2048)
    return xp, xs, k_all, v_all, k_s, v_s, cstate_p, cstate_s


def kernel(x_prompt, x_sample, cache_k, cache_v, state_conv, c_prompt, c_sample, w_mod, b_mod, g_mix, g_ffn, w_in,
           w_dw, b_dw, g_conv_ln, b_conv_ln, w_out, w_peer_q, peer_sub_keys, peer_u, peer_v, g_final):
    depth = w_mod.shape[0]
    batch, seq, _ = x_prompt.shape
    dbatch, dseq, _ = x_sample.shape
    mp, ms = batch * seq, dbatch * dseq

    c_all = jnp.concatenate([c_prompt, c_sample], axis=0)
    c_pad = (-c_all.shape[0]) % 8
    mod = _modulation(jnp.pad(c_all, ((0, c_pad), (0, 0))), w_mod, b_mod)

    xp = x_prompt.reshape(mp, D_MODEL)
    xs = x_sample.reshape(ms, D_MODEL)
    weights = (g_mix, g_ffn, w_in, w_out, w_peer_q, peer_sub_keys, peer_u, peer_v)
    zero_state = jnp.zeros((batch, STATE_PAD, D_CONV), F32)
    k_all = jnp.zeros((depth, mp, D_ATTN), F32)
    v_all = jnp.zeros((depth, mp, D_ATTN), F32)
    outs = {name: [] for name in ("cp", "ks", "vs", "cs")}

    for l in range(depth):
        mod_p = mod[l, :batch].reshape(batch, N_MOD, 1, D_MODEL)
        mods_p = tuple(mod_p[:, i] for i in range(N_MOD))
        mod_s = mod[l, batch:batch + dbatch].reshape(dbatch, N_MOD, D_MODEL)
        mods_s = tuple(jnp.repeat(mod_s[:, i], dseq, axis=0)[None] for i in range(N_MOD))

        conv_args = dict(w_dw=w_dw, b_dw=b_dw, g_ln=g_conv_ln, b_ln=b_conv_ln, layer=l)
        conv_p = functools.partial(_conv_module, state_pad=zero_state, batch=batch, seq=seq, ts=64, out_dtype=BF16,
                                   **conv_args)
        state_s = jnp.pad(state_conv[l], ((0, 0), (STATE_PAD - CONV_STATE, 0), (0, 0)))
        conv_s = functools.partial(_conv_module, state_pad=state_s, batch=dbatch, seq=dseq, ts=dseq, out_dtype=F32,
                                   **conv_args)

        def attn_s(proj, l=l):
            o = _attn_sample(proj.reshape(ms, D_IN // HEAD_DIM, HEAD_DIM), cache_k, cache_v, l, dbatch, dseq)
            return o.reshape(ms, D_ATTN)

        xp, xs, k_all, v_all, k_s, v_s, cstate_p, cstate_s = _layer(
            xp, xs, mods_p, mods_s, seq, functools.partial(_attn_prompt, batch=batch, seq=seq), attn_s,
            conv_p, conv_s, k_all, v_all, l, weights)

        outs["cp"].append(cstate_p)
        outs["ks"].append(k_s.reshape(dbatch, dseq, N_HEADS, HEAD_DIM))
        outs["vs"].append(v_s.reshape(dbatch, dseq, N_HEADS, HEAD_DIM))
        outs["cs"].append(cstate_s)

    keep = min(cache_k.shape[2], seq)
    k_prompt = k_all.reshape(depth, batch, seq, N_HEADS, HEAD_DIM)[:, :, seq - keep:]
    v_prompt = v_all.reshape(depth, batch, seq, N_HEADS, HEAD_DIM)[:, :, seq - keep:]
    y_prompt = _final_norm(xp, g_final, TM_NORM).reshape(batch, seq, D_MODEL)
    y_sample = _final_norm(xs, g_final, ms).reshape(dbatch, dseq, D_MODEL)
    return (y_prompt, y_sample, k_prompt, v_prompt, jnp.stack(outs["cp"]),
            jnp.stack(outs["ks"]), jnp.stack(outs["vs"]), jnp.stack(outs["cs"]))
```

```python
import functools

import jax
import jax.numpy as jnp
from jax import lax
from jax.experimental import pallas as pl
from jax.experimental.pallas import tpu as pltpu

F32 = jnp.float32
BF16 = jnp.bfloat16

D_MODEL = 4096
N_HEADS = 16
HEAD_DIM = 128
D_ATTN = N_HEADS * HEAD_DIM
D_CONV = D_MODEL - D_ATTN
CONV_WIDTH = 31
CONV_STATE = CONV_WIDTH - 1
DILATIONS = (1, 4, 16)
ATTN_BLOCK = 128
D_IN = 3 * D_ATTN + 2 * D_CONV
N_MOD = 6
PEER_HEADS = 8
PEER_TOPK = 16
N_KEYS = 128
N_EXPERTS = N_KEYS * N_KEYS
PEER_QDIM = 256
NORM_EPS = 1e-6
ATTN_SCALE = HEAD_DIM ** -0.5

LANES = 128
SUBLANES = 8
STATE_PAD = 32
VMEM_LIMIT = 56 * 1024 * 1024
NEG_INF = float("-inf")


def _params(*sem):
    return pltpu.CompilerParams(dimension_semantics=sem, vmem_limit_bytes=VMEM_LIMIT)


def _mod_kernel(c_ref, w_ref, b_ref, o_ref):
    c = c_ref[...]
    a = (c * jax.nn.sigmoid(c)).astype(BF16)
    o_ref[...] = jnp.dot(a, w_ref[...].astype(BF16), preferred_element_type=F32) + b_ref[...]


def _modulation(c_all, w_mod, b_mod):
    depth, _, n = w_mod.shape
    r = c_all.shape[0]
    tn = 512
    return pl.pallas_call(
        _mod_kernel,
        grid=(depth, n // tn),
        in_specs=[pl.BlockSpec((r, D_MODEL), lambda l, j: (0, 0)),
                  pl.BlockSpec((None, D_MODEL, tn), lambda l, j: (l, 0, j)),
                  pl.BlockSpec((None, 1, tn), lambda l, j: (l, 0, j))],
        out_specs=pl.BlockSpec((None, r, tn), lambda l, j: (l, 0, j)),
        out_shape=jax.ShapeDtypeStruct((depth, r, n), F32),
        compiler_params=_params("parallel", "parallel"),
        name="modulation",
    )(c_all, w_mod, b_mod.reshape(depth, 1, n))


def _norm_mod_kernel(x_ref, g_ref, sc_ref, sh_ref, o_ref):
    x = x_ref[...]
    y = x * lax.rsqrt(jnp.mean(x * x, axis=-1, keepdims=True) + NORM_EPS) * g_ref[...]
    o_ref[...] = (y * (1.0 + sc_ref[...]) + sh_ref[...]).astype(o_ref.dtype)


def _norm_kernel(x_ref, g_ref, o_ref):
    x = x_ref[...]
    y = x * lax.rsqrt(jnp.mean(x * x, axis=-1, keepdims=True) + NORM_EPS) * g_ref[...]
    o_ref[...] = y.astype(o_ref.dtype)


def _row_spec(arr, tm, tn, tiles_per_group):
    r = arr.shape[1]
    rows = 1 if r == 1 else tm
    if r == 1:
        return pl.BlockSpec((None, rows, tn), lambda i, j: (i // tiles_per_group, 0, j))
    return pl.BlockSpec((None, rows, tn), lambda i, j: (0, i, j))


def _norm_mod(x, g, scale, shift, rows_per_group, tm, out_dtype):
    m = x.shape[0]
    tpg = max(rows_per_group // tm, 1)
    g2 = g.reshape(1, D_MODEL)
    return pl.pallas_call(
        _norm_mod_kernel,
        grid=(m // tm, 1),
        in_specs=[pl.BlockSpec((tm, D_MODEL), lambda i, j: (i, 0)),
                  pl.BlockSpec((1, D_MODEL), lambda i, j: (0, 0)),
                  _row_spec(scale, tm, D_MODEL, tpg),
                  _row_spec(shift, tm, D_MODEL, tpg)],
        out_specs=pl.BlockSpec((tm, D_MODEL), lambda i, j: (i, 0)),
        out_shape=jax.ShapeDtypeStruct((m, D_MODEL), out_dtype),
        compiler_params=_params("parallel", "arbitrary"),
        name="norm_mod",
    )(x, g2, scale, shift)


def _final_norm(x, g, tm):
    m = x.shape[0]
    return pl.pallas_call(
        _norm_kernel,
        grid=(m // tm,),
        in_specs=[pl.BlockSpec((tm, D_MODEL), lambda i: (i, 0)),
                  pl.BlockSpec((1, D_MODEL), lambda i: (0, 0))],
        out_specs=pl.BlockSpec((tm, D_MODEL), lambda i: (i, 0)),
        out_shape=jax.ShapeDtypeStruct((m, D_MODEL), F32),
        compiler_params=_params("parallel"),
        name="final_norm",
    )(x, g.reshape(1, D_MODEL))


def _follow_first_tile(i, j, last):
    return jnp.where(i == 0, j, last)


def _mm_kernel(a_ref, as_ref, w_ref, o_ref, os_ref):
    w = w_ref[...].astype(BF16)
    o_ref[...] = jnp.dot(a_ref[...].astype(BF16), w, preferred_element_type=F32)

    @pl.when(pl.program_id(0) == 0)
    def _():
        os_ref[...] = jnp.dot(as_ref[...].astype(BF16), w, preferred_element_type=F32)


def _matmul(a, a_s, w, layer, tm, tn):
    m, k = a.shape
    ms = a_s.shape[0]
    n = w.shape[2]
    last = n // tn - 1
    return pl.pallas_call(
        _mm_kernel,
        grid=(m // tm, n // tn),
        in_specs=[pl.BlockSpec((tm, k), lambda i, j: (i, 0)),
                  pl.BlockSpec((ms, k), lambda i, j: (0, 0)),
                  pl.BlockSpec((None, k, tn), lambda i, j: (layer, 0, j))],
        out_specs=[pl.BlockSpec((tm, tn), lambda i, j: (i, j)),
                   pl.BlockSpec((ms, tn), lambda i, j: (0, _follow_first_tile(i, j, last)))],
        out_shape=[jax.ShapeDtypeStruct((m, n), F32), jax.ShapeDtypeStruct((ms, n), F32)],
        compiler_params=_params("arbitrary", "arbitrary"),
        name="matmul",
    )(a, a_s, w)


def _in_proj_kernel(a_ref, as_ref, w_ref, kin_ref, vin_ref, o_ref, k_ref, v_ref, os_ref, ks_ref, vs_ref, *, kv_tiles):
    del kin_ref, vin_ref
    j = pl.program_id(1)
    is_k = (j >= kv_tiles) & (j < 2 * kv_tiles)
    is_v = (j >= 2 * kv_tiles) & (j < 3 * kv_tiles)
    w = w_ref[...].astype(BF16)
    acc = jnp.dot(a_ref[...].astype(BF16), w, preferred_element_type=F32)
    o_ref[...] = acc

    @pl.when(is_k)
    def _():
        k_ref[...] = acc

    @pl.when(is_v)
    def _():
        v_ref[...] = acc

    @pl.when(pl.program_id(0) == 0)
    def _():
        acc_s = jnp.dot(as_ref[...].astype(BF16), w, preferred_element_type=F32)
        os_ref[...] = acc_s

        @pl.when(is_k)
        def _():
            ks_ref[...] = acc_s

        @pl.when(is_v)
        def _():
            vs_ref[...] = acc_s


def _in_proj(a, a_s, w, k_all, v_all, layer, tm, tn):
    m, k = a.shape
    ms = a_s.shape[0]
    n = w.shape[2]
    last = n // tn - 1
    kv_tiles = D_ATTN // tn
    kv_col = lambda j, first: jnp.clip(j - first, 0, kv_tiles - 1)

    def kv_spec(first):
        return pl.BlockSpec((None, tm, tn), lambda i, j: (layer, i, kv_col(j, first)))

    def kv_s_spec(first):
        return pl.BlockSpec((ms, tn), lambda i, j: (0, kv_col(_follow_first_tile(i, j, last), first)))

    return pl.pallas_call(
        functools.partial(_in_proj_kernel, kv_tiles=kv_tiles),
        grid=(m // tm, n // tn),
        in_specs=[pl.BlockSpec((tm, k), lambda i, j: (i, 0)),
                  pl.BlockSpec((ms, k), lambda i, j: (0, 0)),
                  pl.BlockSpec((None, k, tn), lambda i, j: (layer, 0, j)),
                  pl.BlockSpec(memory_space=pl.ANY),
                  pl.BlockSpec(memory_space=pl.ANY)],
        out_specs=[pl.BlockSpec((tm, tn), lambda i, j: (i, j)), kv_spec(kv_tiles), kv_spec(2 * kv_tiles),
                   pl.BlockSpec((ms, tn), lambda i, j: (0, _follow_first_tile(i, j, last))),
                   kv_s_spec(kv_tiles), kv_s_spec(2 * kv_tiles)],
        out_shape=[jax.ShapeDtypeStruct((m, n), F32), jax.ShapeDtypeStruct(k_all.shape, F32),
                   jax.ShapeDtypeStruct(v_all.shape, F32), jax.ShapeDtypeStruct((ms, n), F32),
                   jax.ShapeDtypeStruct((ms, D_ATTN), F32), jax.ShapeDtypeStruct((ms, D_ATTN), F32)],
        input_output_aliases={3: 1, 4: 2},
        compiler_params=_params("arbitrary", "arbitrary"),
        name="in_proj",
    )(a, a_s, w, k_all, v_all)


def _mm2_res_kernel(a1_ref, a2_ref, a1s_ref, a2s_ref, w1_ref, w2_ref, x_ref, gate_ref, xs_ref, gates_ref,
                    o_ref, os_ref):
    w1 = w1_ref[...].astype(BF16)
    w2 = w2_ref[...].astype(BF16)
    acc = jnp.dot(a1_ref[...].astype(BF16), w1, preferred_element_type=F32)
    acc = acc + jnp.dot(a2_ref[...].astype(BF16), w2, preferred_element_type=F32)
    o_ref[...] = x_ref[...] + gate_ref[...] * acc

    @pl.when(pl.program_id(0) == 0)
    def _():
        acc_s = jnp.dot(a1s_ref[...].astype(BF16), w1, preferred_element_type=F32)
        acc_s = acc_s + jnp.dot(a2s_ref[...].astype(BF16), w2, preferred_element_type=F32)
        os_ref[...] = xs_ref[...] + gates_ref[...] * acc_s


def _matmul_concat_res(a1, a2, a1_s, a2_s, w, layer, tm, tn, x, gate, x_s, gate_s, rows_per_group):
    m, k1 = a1.shape
    k2 = a2.shape[1]
    ms = a1_s.shape[0]
    assert k1 == k2 and w.shape[1] == k1 + k2
    n = w.shape[2]
    last = n // tn - 1
    tpg = max(rows_per_group // tm, 1)
    s_col = lambda i, j: (0, _follow_first_tile(i, j, last))
    return pl.pallas_call(
        _mm2_res_kernel,
        grid=(m // tm, n // tn),
        in_specs=[pl.BlockSpec((tm, k1), lambda i, j: (i, 0)),
                  pl.BlockSpec((tm, k2), lambda i, j: (i, 0)),
                  pl.BlockSpec((ms, k1), lambda i, j: (0, 0)),
                  pl.BlockSpec((ms, k2), lambda i, j: (0, 0)),
                  pl.BlockSpec((None, k1, tn), lambda i, j: (layer, 0, j)),
                  pl.BlockSpec((None, k2, tn), lambda i, j: (layer, 1, j)),
                  pl.BlockSpec((tm, tn), lambda i, j: (i, j)),
                  _row_spec(gate, tm, tn, tpg),
                  pl.BlockSpec((ms, tn), s_col),
                  pl.BlockSpec((None, ms, tn), lambda i, j: (0,) + s_col(i, j))],
        out_specs=[pl.BlockSpec((tm, tn), lambda i, j: (i, j)), pl.BlockSpec((ms, tn), s_col)],
        out_shape=[jax.ShapeDtypeStruct((m, n), F32), jax.ShapeDtypeStruct((ms, n), F32)],
        compiler_params=_params("arbitrary", "arbitrary"),
        name="matmul_concat",
    )(a1, a2, a1_s, a2_s, w, w, x, gate, x_s, gate_s)


ATTN_GROUP = 8


def _attn_prompt_kernel(q_ref, k_ref, v_ref, o_ref, o_sc, lse_sc):
    seq = q_ref.shape[0]
    nt = (((1,), (1,)), ((), ()))
    row = lax.broadcasted_iota(jnp.int32, (ATTN_BLOCK, 2 * ATTN_BLOCK), 0)
    col = lax.broadcasted_iota(jnp.int32, (ATTN_BLOCK, 2 * ATTN_BLOCK), 1)
    band_mask = (col >= row) & (col <= row + ATTN_BLOCK)
    causal_mask = (lax.broadcasted_iota(jnp.int32, (ATTN_BLOCK, ATTN_BLOCK), 1)
                   <= lax.broadcasted_iota(jnp.int32, (ATTN_BLOCK, ATTN_BLOCK), 0))

    def rows_of(start, n, dil):
        return pl.ds(start, n, stride=dil) if dil > 1 else pl.ds(start, n)

    def run_group(p, dil, units):
        qs, ks, vs, masks = [], [], [], []
        for start, has_prev in units:
            nk = 2 * ATTN_BLOCK if has_prev else ATTN_BLOCK
            kstart = start - ATTN_BLOCK * dil if has_prev else start
            qs.append(q_ref[rows_of(start, ATTN_BLOCK, dil), :].astype(BF16))
            ks.append(k_ref[rows_of(kstart, nk, dil), :].astype(BF16))
            v = v_ref[rows_of(kstart, nk, dil), :].astype(BF16)
            vs.append(jnp.concatenate([v, jnp.ones((nk, HEAD_DIM), BF16)], axis=-1))
            masks.append(band_mask if has_prev else causal_mask)
        ss = [jnp.where(mk, lax.dot_general(q, k, nt, preferred_element_type=F32) * ATTN_SCALE, NEG_INF)
              for q, k, mk in zip(qs, ks, masks)]
        ms = [jnp.max(s, axis=-1, keepdims=True) for s in ss]
        ps = [jnp.exp(s - m).astype(BF16) for s, m in zip(ss, ms)]
        rs = [jnp.dot(pr, v, preferred_element_type=F32) for pr, v in zip(ps, vs)]
        for (start, _), r, m in zip(units, rs, ms):
            acc, den = r[:, :HEAD_DIM], r[:, HEAD_DIM:]
            rows = rows_of(start, ATTN_BLOCK, dil)
            o_sc[p, rows, :] = acc / den
            lse_sc[p, rows, :] = m + jnp.log(den)

    for p, dil in enumerate(DILATIONS):
        nblk = seq // dil // ATTN_BLOCK
        units = [(r + blk * ATTN_BLOCK * dil, blk > 0) for blk in range(nblk) for r in range(dil)]
        for g0 in range(0, len(units), ATTN_GROUP):
            run_group(p, dil, units[g0:g0 + ATTN_GROUP])

    l0, l1, l2 = lse_sc[0], lse_sc[1], lse_sc[2]
    mx = jnp.maximum(jnp.maximum(l0, l1), l2)
    w0, w1, w2 = jnp.exp(l0 - mx), jnp.exp(l1 - mx), jnp.exp(l2 - mx)
    out = (w0 * o_sc[0] + w1 * o_sc[1] + w2 * o_sc[2]) / (w0 + w1 + w2)
    o_ref[...] = out.astype(o_ref.dtype)


def _attn_prompt(proj, batch, seq):
    blk = (seq, HEAD_DIM)
    return pl.pallas_call(
        _attn_prompt_kernel,
        grid=(batch, N_HEADS),
        in_specs=[pl.BlockSpec(blk, lambda b, h: (b, h)),
                  pl.BlockSpec(blk, lambda b, h: (b, N_HEADS + h)),
                  pl.BlockSpec(blk, lambda b, h: (b, 2 * N_HEADS + h))],
        out_specs=pl.BlockSpec(blk, lambda b, h: (b, h)),
        out_shape=jax.ShapeDtypeStruct((batch * seq, D_ATTN), BF16),
        scratch_shapes=[pltpu.VMEM((len(DILATIONS), seq, HEAD_DIM), F32),
                        pltpu.VMEM((len(DILATIONS), seq, HEAD_DIM), F32)],
        compiler_params=_params("parallel", "parallel"),
        name="attn_prompt",
    )(proj, proj, proj)


def _pattern_count(dist):
    cnt = jnp.zeros(dist.shape, F32)
    for dil in DILATIONS:
        ok = (dist >= 0) & ((dist & (dil - 1)) == 0) & (dist <= ATTN_BLOCK * dil)
        cnt = cnt + jnp.where(ok, 1.0, 0.0)
    return cnt


def _attn_sample_kernel(q_ref, kn_ref, vn_ref, kc_ref, vc_ref, o_ref, m_sc, l_sc, acc_sc, cnt_sc, *, cache_len, chunk):
    b = pl.program_id(0)
    c = pl.program_id(1)
    t_new = q_ref.shape[0]
    rows_n = t_new * N_HEADS
    nt = (((1,), (1,)), ((), ()))
    q = q_ref[...].reshape(rows_n, HEAD_DIM).astype(BF16)

    def pattern_weights(n, key_pos):
        ri = lax.broadcasted_iota(jnp.int32, (rows_n, n), 0)
        ci = lax.broadcasted_iota(jnp.int32, (rows_n, n), 1)
        same_head = (ri & (N_HEADS - 1)) == (ci & (N_HEADS - 1))
        dist = cache_len + (ri >> 4) - (key_pos + (ci >> 4))
        return jnp.where(same_head, _pattern_count(dist), 0.0)

    def scores(k2d, cnt):
        s = lax.dot_general(q, k2d.astype(BF16), nt, preferred_element_type=F32) * ATTN_SCALE
        return jnp.where(cnt > 0.0, s, NEG_INF)

    @pl.when(b == 0)
    def _():
        cnt_sc[c] = pattern_weights(chunk * N_HEADS, c * chunk)

    @pl.when(c == 0)
    def _():
        cnt = pattern_weights(rows_n, cache_len)
        s = scores(kn_ref[...].reshape(rows_n, HEAD_DIM), cnt)
        m = jnp.max(s, axis=-1, keepdims=True)
        p = cnt * jnp.exp(s - m)
        m_sc[...] = m
        l_sc[...] = jnp.sum(p, axis=-1, keepdims=True)
        acc_sc[...] = jnp.dot(p.astype(BF16), vn_ref[...].reshape(rows_n, HEAD_DIM).astype(BF16),
                              preferred_element_type=F32)

    cnt = cnt_sc[c]
    s = scores(kc_ref[...].reshape(chunk * N_HEADS, HEAD_DIM), cnt)
    m_old = m_sc[...]
    m_new = jnp.maximum(m_old, jnp.max(s, axis=-1, keepdims=True))
    alpha = jnp.exp(m_old - m_new)
    p = cnt * jnp.exp(s - m_new)
    l_sc[...] = alpha * l_sc[...] + jnp.sum(p, axis=-1, keepdims=True)
    acc_sc[...] = alpha * acc_sc[...] + jnp.dot(
        p.astype(BF16), vc_ref[...].reshape(chunk * N_HEADS, HEAD_DIM).astype(BF16), preferred_element_type=F32)
    m_sc[...] = m_new

    @pl.when(c == pl.num_programs(1) - 1)
    def _():
        o_ref[...] = (acc_sc[...] / l_sc[...]).reshape(t_new, N_HEADS, HEAD_DIM)


def _attn_sample(proj3, cache_k, cache_v, layer, batch, t_new):
    cache_len = cache_k.shape[2]
    chunk = 256
    rows_n = t_new * N_HEADS
    new_blk = (t_new, N_HEADS, HEAD_DIM)
    cache_blk = (None, None, chunk, N_HEADS, HEAD_DIM)
    return pl.pallas_call(
        functools.partial(_attn_sample_kernel, cache_len=cache_len, chunk=chunk),
        grid=(batch, cache_len // chunk),
        in_specs=[pl.BlockSpec(new_blk, lambda b, c: (b, 0, 0)),
                  pl.BlockSpec(new_blk, lambda b, c: (b, 1, 0)),
                  pl.BlockSpec(new_blk, lambda b, c: (b, 2, 0)),
                  pl.BlockSpec(cache_blk, lambda b, c: (layer, b, c, 0, 0)),
                  pl.BlockSpec(cache_blk, lambda b, c: (layer, b, c, 0, 0))],
        out_specs=pl.BlockSpec(new_blk, lambda b, c: (b, 0, 0)),
        out_shape=jax.ShapeDtypeStruct((batch * t_new, N_HEADS, HEAD_DIM), F32),
        scratch_shapes=[pltpu.VMEM((rows_n, 1), F32), pltpu.VMEM((rows_n, 1), F32),
                        pltpu.VMEM((rows_n, HEAD_DIM), F32),
                        pltpu.VMEM((cache_len // chunk, rows_n, chunk * N_HEADS), F32)],
        compiler_params=_params("arbitrary", "arbitrary"),
        name="attn_sample",
    )(proj3, proj3, proj3, cache_k, cache_v)


def _conv_kernel(ga_ref, gb_ref, st_ref, wdw_ref, bdw_ref, g_ref, b_ref, o_ref, so_ref, ubuf, cbuf, *, ts, lane_chunk):
    j = pl.program_id(1)
    first = STATE_PAD - CONV_STATE

    @pl.when(j == 0)
    def _():
        ubuf[0:STATE_PAD, :] = st_ref[...]
        ubuf[STATE_PAD + ts:STATE_PAD + ts + SUBLANES, :] = jnp.zeros((SUBLANES, D_CONV), F32)

    gb = gb_ref[...]
    ubuf[STATE_PAD:STATE_PAD + ts, :] = ga_ref[...] * jax.nn.sigmoid(gb)

    for c0 in range(0, D_CONV, lane_chunk):
        lanes = slice(c0, c0 + lane_chunk)
        acc = jnp.broadcast_to(bdw_ref[:, lanes], (ts, lane_chunk))
        for rho in range(SUBLANES):
            part = None
            for base in range(0, STATE_PAD + SUBLANES, SUBLANES):
                w = base + rho - first
                if 0 <= w < CONV_WIDTH:
                    term = ubuf[base:base + ts + SUBLANES, lanes] * wdw_ref[w:w + 1, lanes]
                    part = term if part is None else part + term
            acc = acc + part[rho:rho + ts, :]
        cbuf[:, lanes] = acc

    conv = cbuf[...]
    mu = jnp.mean(conv, axis=-1, keepdims=True)
    xc = conv - mu
    y = xc * lax.rsqrt(jnp.mean(xc * xc, axis=-1, keepdims=True) + NORM_EPS)
    y = y * g_ref[...] + b_ref[...]
    o_ref[...] = (y * jax.nn.sigmoid(y)).astype(o_ref.dtype)

    @pl.when(j == pl.num_programs(1) - 1)
    def _():
        so_ref[...] = ubuf[ts + first:ts + STATE_PAD, :]

    tail = ubuf[ts:ts + STATE_PAD, :]
    ubuf[0:STATE_PAD, :] = tail


def _conv_module(proj, state_pad, w_dw, b_dw, g_ln, b_ln, layer, batch, seq, ts, out_dtype):
    steps = seq // ts
    ga_col = 3 * D_ATTN // D_CONV
    vec = lambda a: a.reshape(a.shape[0], 1, D_CONV)
    vspec = pl.BlockSpec((None, 1, D_CONV), lambda b, j: (layer, 0, 0))
    return pl.pallas_call(
        functools.partial(_conv_kernel, ts=ts, lane_chunk=512),
        grid=(batch, steps),
        in_specs=[pl.BlockSpec((ts, D_CONV), lambda b, j: (b * steps + j, ga_col)),
                  pl.BlockSpec((ts, D_CONV), lambda b, j: (b * steps + j, ga_col + 1)),
                  pl.BlockSpec((None, STATE_PAD, D_CONV), lambda b, j: (b, 0, 0)),
                  pl.BlockSpec((None, CONV_WIDTH, D_CONV), lambda b, j: (layer, 0, 0)),
                  vspec, vspec, vspec],
        out_specs=[pl.BlockSpec((ts, D_CONV), lambda b, j: (b * steps + j, 0)),
                   pl.BlockSpec((None, CONV_STATE, D_CONV), lambda b, j: (b, 0, 0))],
        out_shape=[jax.ShapeDtypeStruct((batch * seq, D_CONV), out_dtype),
                   jax.ShapeDtypeStruct((batch, CONV_STATE, D_CONV), F32)],
        scratch_shapes=[pltpu.VMEM((STATE_PAD + ts + SUBLANES, D_CONV), F32), pltpu.VMEM((ts, D_CONV), F32)],
        compiler_params=_params("parallel", "arbitrary"),
        name="conv_module",
    )(proj, proj, state_pad, w_dw, vec(b_dw), vec(g_ln), vec(b_ln))


CAND_PAD = 1.0e9
TOKEN_UNROLL = 32
HEAD_GROUP = 8


def _cand_plan():
    tiles = [("a", 0, 0), ("a", 0, SUBLANES), ("a", 1, 0), ("b", 0, SUBLANES)] + [("b", b, 0) for b in range(5)]
    flat, seen = [], set()
    for kind, fixed, start in tiles:
        for r in range(SUBLANES):
            a, b = (fixed, start + r) if kind == "a" else (start + r, fixed)
            if (a + 1) * (b + 1) <= PEER_TOPK and (a, b) not in seen:
                seen.add((a, b))
                flat.append(float(a * PEER_TOPK + b))
            else:
                flat.append(CAND_PAD)
    need = {(a, b) for a in range(PEER_TOPK) for b in range(PEER_TOPK) if (a + 1) * (b + 1) <= PEER_TOPK}
    assert seen == need
    return tiles, flat


def _peer_topk_kernel(q_ref, keys_ref, flat_ref, g_ref, s_sc, v_sc, i_sc, cand_sc, cid_sc, top_sc,
                      e_sc, w_sc, i1t_sc, i2t_sc, wt_sc, *, tiles):
    tm = q_ref.shape[0]
    nt = (((1,), (1,)), ((), ()))
    iota_keys = lax.broadcasted_iota(jnp.int32, (N_KEYS, tm), 0).astype(F32)
    flat = flat_ref[...]
    valid = flat < CAND_PAD

    def head_group(hg, carry):
        for hh in range(HEAD_GROUP):
            h = hg * HEAD_GROUP + hh
            col = pl.multiple_of(h * PEER_QDIM, PEER_QDIM)
            for half in range(2):
                qh = q_ref[:, pl.ds(col + half * N_KEYS, N_KEYS)].astype(BF16)
                kh = keys_ref[h, half].astype(BF16)
                s_sc[hh * 2 + half] = lax.dot_general(kh, qh, nt, preferred_element_type=F32)

        def key_step(k, c):
            for ch in range(2 * HEAD_GROUP):
                s = s_sc[ch]
                m = jnp.max(s, axis=0, keepdims=True)
                ix = jnp.min(jnp.where(s == m, iota_keys, float(N_KEYS)), axis=0, keepdims=True)
                v_sc[ch, pl.ds(k, 1), :] = m
                i_sc[ch, pl.ds(k, 1), :] = ix
                s_sc[ch] = jnp.where(iota_keys == ix, NEG_INF, s)
            return c

        lax.fori_loop(0, PEER_TOPK, key_step, 0)

        for hh in range(HEAD_GROUP):
            v1, v2, i1, i2 = v_sc[2 * hh], v_sc[2 * hh + 1], i_sc[2 * hh], i_sc[2 * hh + 1]
            sums, ids = [], []
            for kind, fixed, start in tiles:
                one, rng = slice(fixed, fixed + 1), slice(start, start + SUBLANES)
                ra, rb = (one, rng) if kind == "a" else (rng, one)
                sums.append(v1[ra, :] + v2[rb, :])
                ids.append(i1[ra, :] * float(N_KEYS) + i2[rb, :])
            cand_sc[hh] = jnp.where(valid, jnp.concatenate(sums, axis=0), NEG_INF)
            cid_sc[hh] = jnp.concatenate(ids, axis=0)

        def cand_step(k, c):
            for hh in range(HEAD_GROUP):
                cnd = cand_sc[hh]
                m = jnp.max(cnd, axis=0, keepdims=True)
                pos = jnp.min(jnp.where(cnd == m, flat, 2.0 * CAND_PAD), axis=0, keepdims=True)
                hit = flat == pos
                top_sc[hh, pl.ds(k, 1), :] = m
                e_sc[pl.ds((hg * HEAD_GROUP + hh) * PEER_TOPK + k, 1), :] = jnp.max(
                    jnp.where(hit, cid_sc[hh], -1.0), axis=0, keepdims=True)
                cand_sc[hh] = jnp.where(hit, NEG_INF, cnd)
            return c

        lax.fori_loop(0, PEER_TOPK, cand_step, 0)

        for hh in range(HEAD_GROUP):
            top = top_sc[hh]
            ex = jnp.exp(top - top[0:1, :])
            out_rows = pl.ds(pl.multiple_of((hg * HEAD_GROUP + hh) * PEER_TOPK, PEER_TOPK), PEER_TOPK)
            w_sc[out_rows, :] = ex / jnp.sum(ex, axis=0, keepdims=True)
        return carry

    lax.fori_loop(0, PEER_HEADS // HEAD_GROUP, head_group, 0)

    e = e_sc[...]
    i1 = jnp.floor(e * (1.0 / N_KEYS))
    i1t_sc[...] = jnp.transpose(i1)
    i2t_sc[...] = jnp.transpose(e - i1 * float(N_KEYS))
    wt_sc[...] = jnp.transpose(w_sc[...])

    sub = lax.broadcasted_iota(jnp.int32, (N_KEYS, PEER_HEADS * PEER_TOPK), 0).astype(F32)

    def token(t, carry):
        i1row = i1t_sc[pl.ds(t, 1), :]
        i2row = i2t_sc[pl.ds(t, 1), :]
        wrow = wt_sc[pl.ds(t, 1), :]
        lmat = jnp.where(sub == i1row, wrow, 0.0).astype(BF16)
        rmat = jnp.where(sub == i2row, 1.0, 0.0).astype(BF16)
        g_ref[t] = lax.dot_general(lmat, rmat, nt, preferred_element_type=F32)
        return carry

    lax.fori_loop(0, tm, token, 0, unroll=TOKEN_UNROLL)


def _peer_topk(q, sub_keys, layer):
    t = q.shape[0]
    tm = LANES
    slots = PEER_HEADS * PEER_TOPK
    tiles, flat = _cand_plan()
    ncand = len(flat)
    flat_arr = jnp.broadcast_to(jnp.asarray(flat, F32)[:, None], (ncand, tm))
    sc = lambda *shape: pltpu.VMEM(shape + (tm,), F32)
    return pl.pallas_call(
        functools.partial(_peer_topk_kernel, tiles=tiles),
        grid=(t // tm,),
        in_specs=[pl.BlockSpec((tm, PEER_HEADS * PEER_QDIM), lambda i: (i, 0)),
                  pl.BlockSpec((None, PEER_HEADS, 2, N_KEYS, PEER_QDIM // 2), lambda i: (layer, 0, 0, 0, 0)),
                  pl.BlockSpec((ncand, tm), lambda i: (0, 0))],
        out_specs=pl.BlockSpec((tm, N_KEYS, N_KEYS), lambda i: (i, 0, 0)),
        out_shape=jax.ShapeDtypeStruct((t, N_KEYS, N_KEYS), F32),
        scratch_shapes=[sc(2 * HEAD_GROUP, N_KEYS), sc(2 * HEAD_GROUP, PEER_TOPK), sc(2 * HEAD_GROUP, PEER_TOPK),
                        sc(HEAD_GROUP, ncand), sc(HEAD_GROUP, ncand), sc(HEAD_GROUP, PEER_TOPK),
                        sc(slots), sc(slots),
                        pltpu.VMEM((tm, slots), F32), pltpu.VMEM((tm, slots), F32), pltpu.VMEM((tm, slots), F32)],
        compiler_params=_params("parallel"),
        name="peer_topk",
    )(q, sub_keys, flat_arr)


def _gelu(x):
    return 0.5 * x * (1.0 + lax.erf(x * (2.0 ** -0.5)))


G_ROWS = 8


def _gated_gelu_tile(h_bf16, u_rows, g2, first_row, n_rows, n_sub):
    nt = (((1,), (1,)), ((), ()))
    act = _gelu(lax.dot_general(h_bf16, u_rows.astype(BF16), nt, preferred_element_type=F32))
    return [(g2[pl.ds(first_row + c, n_rows, stride=G_ROWS), :] * act[:, c * N_KEYS:(c + 1) * N_KEYS]).astype(BF16)
            for c in range(n_sub)]


def _peer_up_kernel(h_ref, hs_ref, u_ref, g_ref, gs_ref, o_ref, os_ref, *, nsub):
    tm, ms = h_ref.shape[0], hs_ref.shape[0]
    g2 = g_ref.reshape(tm * G_ROWS, N_KEYS)
    gs2 = gs_ref.reshape(ms * G_ROWS, N_KEYS)
    first_row = (pl.program_id(1) % (G_ROWS // nsub)) * nsub
    hb = h_ref[...].astype(BF16)
    half = nsub // 2
    for c0 in range(0, nsub, half):
        pieces = _gated_gelu_tile(hb, u_ref[c0 * N_KEYS:(c0 + half) * N_KEYS, :], g2, first_row + c0, tm, half)
        for c, piece in enumerate(pieces):
            o_ref[:, (c0 + c) * N_KEYS:(c0 + c + 1) * N_KEYS] = piece

    @pl.when(pl.program_id(0) == 0)
    def _():
        pieces = _gated_gelu_tile(hs_ref[...].astype(BF16), u_ref[...], gs2, first_row, ms, nsub)
        for c, piece in enumerate(pieces):
            os_ref[:, c * N_KEYS:(c + 1) * N_KEYS] = piece


def _peer_up(h, h_s, peer_u, g3, g3_s, layer, tm, tn):
    m, ms = h.shape[0], h_s.shape[0]
    nsub = tn // N_KEYS
    phases = G_ROWS // nsub
    last = N_EXPERTS // tn - 1
    return pl.pallas_call(
        functools.partial(_peer_up_kernel, nsub=nsub),
        grid=(m // tm, N_EXPERTS // tn),
        in_specs=[pl.BlockSpec((tm, D_MODEL), lambda i, j: (i, 0)),
                  pl.BlockSpec((ms, D_MODEL), lambda i, j: (0, 0)),
                  pl.BlockSpec((None, tn, D_MODEL), lambda i, j: (layer, j, 0)),
                  pl.BlockSpec((tm, G_ROWS, N_KEYS), lambda i, j: (i, j // phases, 0)),
                  pl.BlockSpec((ms, G_ROWS, N_KEYS), lambda i, j: (0, _follow_first_tile(i, j, last) // phases, 0))],
        out_specs=[pl.BlockSpec((tm, tn), lambda i, j: (i, j)),
                   pl.BlockSpec((ms, tn), lambda i, j: (0, _follow_first_tile(i, j, last)))],
        out_shape=[jax.ShapeDtypeStruct((m, N_EXPERTS), BF16), jax.ShapeDtypeStruct((ms, N_EXPERTS), BF16)],
        compiler_params=_params("arbitrary", "arbitrary"),
        name="peer_up",
    )(h, h_s, peer_u, g3, g3_s)


def _peer_down_kernel(w_ref, ws_ref, v_ref, x_ref, gate_ref, xs_ref, gates_ref, o_ref, os_ref):
    k = pl.program_id(2)
    first, last = k == 0, k == pl.num_programs(2) - 1

    def accumulate(lhs_ref, res_ref, gate, acc_ref):
        @pl.when(first)
        def _():
            acc_ref[...] = jnp.zeros_like(acc_ref)

        acc_ref[...] += jnp.dot(lhs_ref[...], v_ref[...].astype(BF16), preferred_element_type=F32)

        @pl.when(last)
        def _():
            acc_ref[...] = res_ref[...] + gate[...] * acc_ref[...]

    accumulate(w_ref, x_ref, gate_ref, o_ref)

    @pl.when(pl.program_id(0) == 0)
    def _():
        accumulate(ws_ref, xs_ref, gates_ref, os_ref)


def _peer_down(w, w_s, peer_v, x, gate, x_s, gate_s, layer, rows_per_group, tm, tn, tk):
    m, ms = w.shape[0], w_s.shape[0]
    tpg = max(rows_per_group // tm, 1)
    last_j, last_k = D_MODEL // tn - 1, N_EXPERTS // tk - 1
    s_col = lambda i, j, k: (0, _follow_first_tile(i, j, last_j))
    return pl.pallas_call(
        _peer_down_kernel,
        grid=(m // tm, D_MODEL // tn, N_EXPERTS // tk),
        in_specs=[pl.BlockSpec((tm, tk), lambda i, j, k: (i, k)),
                  pl.BlockSpec((ms, tk), lambda i, j, k: (0, _follow_first_tile(i, k, last_k))),
                  pl.BlockSpec((None, tk, tn), lambda i, j, k: (layer, k, j)),
                  pl.BlockSpec((tm, tn), lambda i, j, k: (i, j)),
                  pl.BlockSpec((None, 1, tn), lambda i, j, k: (i // tpg, 0, j)),
                  pl.BlockSpec((ms, tn), s_col),
                  pl.BlockSpec((None, ms, tn), lambda i, j, k: (0,) + s_col(i, j, k))],
        out_specs=[pl.BlockSpec((tm, tn), lambda i, j, k: (i, j)), pl.BlockSpec((ms, tn), s_col)],
        out_shape=[jax.ShapeDtypeStruct((m, D_MODEL), F32), jax.ShapeDtypeStruct((ms, D_MODEL), F32)],
        compiler_params=_params("arbitrary", "arbitrary", "arbitrary"),
        name="peer_down",
    )(w, w_s, peer_v, x, gate, x_s, gate_s)


TM = 1024
TN = 512
TM_NORM = 512
CONV_ROWS = 256


def _layer(xp, xs, mods_p, mods_s, seq, attn_p_fn, attn_s_fn, conv_p_fn, conv_s_fn, k_all, v_all, layer, weights):
    (g_mix, g_ffn, w_in, w_out, w_peer_q, peer_sub_keys, peer_u, peer_v) = weights
    shift1, scale1, gate1, shift2, scale2, gate2 = mods_p
    shift1s, scale1s, gate1s, shift2s, scale2s, gate2s = mods_s
    ms = xs.shape[0]

    hp = _norm_mod(xp, g_mix[layer], scale1, shift1, seq, TM_NORM, BF16)
    hs = _norm_mod(xs, g_mix[layer], scale1s, shift1s, ms, ms, F32)
    proj_p, k_all, v_all, proj_s, k_s, v_s = _in_proj(hp, hs, w_in, k_all, v_all, layer, TM, TN)
    conv_p, cstate_p = conv_p_fn(proj_p)
    conv_s, cstate_s = conv_s_fn(proj_s)
    xp, xs = _matmul_concat_res(attn_p_fn(proj_p), conv_p, attn_s_fn(proj_s), conv_s, w_out, layer, TM, TN,
                                xp, gate1, xs, gate1s, seq)

    h2p = _norm_mod(xp, g_ffn[layer], scale2, shift2, seq, TM_NORM, BF16)
    h2s = _norm_mod(xs, g_ffn[layer], scale2s, shift2s, ms, ms, F32)
    qp, qs = _matmul(h2p, h2s, w_peer_q, layer, TM, TN)
    g3p = _peer_topk(qp, peer_sub_keys, layer)
    g3s = _peer_topk(jnp.pad(qs, ((0, (-ms) % LANES), (0, 0))), peer_sub_keys, layer)
    wp, ws = _peer_up(h2p, h2s, peer_u, g3p, g3s, layer, TM, TN)
    xp, xs = _peer_down(wp, ws, peer_v, xp, gate2, xs, gate2s, layer, seq, TM, 1024, 2048)
    return xp, xs, k_all, v_all, k_s, v_s, cstate_p, cstate_s


def kernel(x_prompt, x_sample, cache_k, cache_v, state_conv, c_prompt, c_sample, w_mod, b_mod, g_mix, g_ffn, w_in,
           w_dw, b_dw, g_conv_ln, b_conv_ln, w_out, w_peer_q, peer_sub_keys, peer_u, peer_v, g_final):
    depth = w_mod.shape[0]
    batch, seq, _ = x_prompt.shape
    dbatch, dseq, _ = x_sample.shape
    mp, ms = batch * seq, dbatch * dseq

    c_all = jnp.concatenate([c_prompt, c_sample], axis=0)
    c_pad = (-c_all.shape[0]) % 8
    mod = _modulation(jnp.pad(c_all, ((0, c_pad), (0, 0))), w_mod, b_mod)

    xp = x_prompt.reshape(mp, D_MODEL)
    xs = x_sample.reshape(ms, D_MODEL)
    weights = (g_mix, g_ffn, w_in, w_out, w_peer_q, peer_sub_keys, peer_u, peer_v)
    zero_state = jnp.zeros((batch, STATE_PAD, D_CONV), F32)
    k_all = jnp.zeros((depth, mp, D_ATTN), F32)
    v_all = jnp.zeros((depth, mp, D_ATTN), F32)
    outs = {name: [] for name in ("cp", "ks", "vs", "cs")}

    for l in range(depth):
        mod_p = mod[l, :batch].reshape(batch, N_MOD, 1, D_MODEL)
        mods_p = tuple(mod_p[:, i] for i in range(N_MOD))
        mod_s = mod[l, batch:batch + dbatch].reshape(dbatch, N_MOD, D_MODEL)
        mods_s = tuple(jnp.repeat(mod_s[:, i], dseq, axis=0)[None] for i in range(N_MOD))

        conv_args = dict(w_dw=w_dw, b_dw=b_dw, g_ln=g_conv_ln, b_ln=b_conv_ln, layer=l)
        conv_p = functools.partial(_conv_module, state_pad=zero_state, batch=batch, seq=seq, ts=CONV_ROWS, out_dtype=BF16,
                                   **conv_args)
        state_s = jnp.pad(state_conv[l], ((0, 0), (STATE_PAD - CONV_STATE, 0), (0, 0)))
        conv_s = functools.partial(_conv_module, state_pad=state_s, batch=dbatch, seq=dseq, ts=dseq, out_dtype=F32,
                                   **conv_args)

        def attn_s(proj, l=l):
            o = _attn_sample(proj.reshape(ms, D_IN // HEAD_DIM, HEAD_DIM), cache_k, cache_v, l, dbatch, dseq)
            return o.reshape(ms, D_ATTN)

        xp, xs, k_all, v_all, k_s, v_s, cstate_p, cstate_s = _layer(
            xp, xs, mods_p, mods_s, seq, functools.partial(_attn_prompt, batch=batch, seq=seq), attn_s,
            conv_p, conv_s, k_all, v_all, l, weights)

        outs["cp"].append(cstate_p)
        outs["ks"].append(k_s.reshape(dbatch, dseq, N_HEADS, HEAD_DIM))
        outs["vs"].append(v_s.reshape(dbatch, dseq, N_HEADS, HEAD_DIM))
        outs["cs"].append(cstate_s)

    keep = min(cache_k.shape[2], seq)
    k_prompt = k_all.reshape(depth, batch, seq, N_HEADS, HEAD_DIM)[:, :, seq - keep:]
    v_prompt = v_all.reshape(depth, batch, seq, N_HEADS, HEAD_DIM)[:, :, seq - keep:]
    y_prompt = _final_norm(xp, g_final, TM_NORM).reshape(batch, seq, D_MODEL)
    y_sample = _final_norm(xs, g_final, ms).reshape(dbatch, dseq, D_MODEL)
    return (y_prompt, y_sample, k_prompt, v_prompt, jnp.stack(outs["cp"]),
            jnp.stack(outs["ks"]), jnp.stack(outs["vs"]), jnp.stack(outs["cs"]))
```

```python
import functools

import jax
import jax.numpy as jnp
from jax import lax
from jax.experimental import pallas as pl
from jax.experimental.pallas import tpu as pltpu

F32 = jnp.float32
BF16 = jnp.bfloat16

D_MODEL = 4096
N_HEADS = 16
HEAD_DIM = 128
D_ATTN = N_HEADS * HEAD_DIM
D_CONV = D_MODEL - D_ATTN
CONV_WIDTH = 31
CONV_STATE = CONV_WIDTH - 1
DILATIONS = (1, 4, 16)
ATTN_BLOCK = 128
D_IN = 3 * D_ATTN + 2 * D_CONV
N_MOD = 6
PEER_HEADS = 8
PEER_TOPK = 16
N_KEYS = 128
N_EXPERTS = N_KEYS * N_KEYS
PEER_QDIM = 256
NORM_EPS = 1e-6
ATTN_SCALE = HEAD_DIM ** -0.5

LANES = 128
SUBLANES = 8
STATE_PAD = 32
VMEM_LIMIT = 56 * 1024 * 1024

TM = 1024
TN = 512
DOWN_TN = 1024
DOWN_TK = 2048
MOD_TN = 512
TM_NORM = 512
CONV_ROWS = 256
CONV_LANES = 512
CACHE_CHUNK = 256
HEAD_SHIFT = N_HEADS.bit_length() - 1
NEG_INF = float("-inf")


def _params(*sem):
    return pltpu.CompilerParams(dimension_semantics=sem, vmem_limit_bytes=VMEM_LIMIT)


def _mod_kernel(c_ref, w_ref, b_ref, o_ref):
    c = c_ref[...]
    a = (c * jax.nn.sigmoid(c)).astype(BF16)
    o_ref[...] = jnp.dot(a, w_ref[...].astype(BF16), preferred_element_type=F32) + b_ref[...]


def _modulation(c_all, w_mod, b_mod):
    depth, _, n = w_mod.shape
    r = c_all.shape[0]
    tn = MOD_TN
    return pl.pallas_call(
        _mod_kernel,
        grid=(depth, n // tn),
        in_specs=[pl.BlockSpec((r, D_MODEL), lambda l, j: (0, 0)),
                  pl.BlockSpec((None, D_MODEL, tn), lambda l, j: (l, 0, j)),
                  pl.BlockSpec((None, 1, tn), lambda l, j: (l, 0, j))],
        out_specs=pl.BlockSpec((None, r, tn), lambda l, j: (l, 0, j)),
        out_shape=jax.ShapeDtypeStruct((depth, r, n), F32),
        compiler_params=_params("parallel", "parallel"),
        name="modulation",
    )(c_all, w_mod, b_mod.reshape(depth, 1, n))


def _norm_mod_kernel(x_ref, g_ref, sc_ref, sh_ref, o_ref):
    x = x_ref[...]
    y = x * lax.rsqrt(jnp.mean(x * x, axis=-1, keepdims=True) + NORM_EPS) * g_ref[...]
    o_ref[...] = (y * (1.0 + sc_ref[...]) + sh_ref[...]).astype(o_ref.dtype)


def _norm_kernel(x_ref, g_ref, o_ref):
    x = x_ref[...]
    y = x * lax.rsqrt(jnp.mean(x * x, axis=-1, keepdims=True) + NORM_EPS) * g_ref[...]
    o_ref[...] = y.astype(o_ref.dtype)


def _row_spec(arr, tm, tn, tiles_per_group):
    if arr.shape[1] == 1:
        return pl.BlockSpec((None, 1, tn), lambda i, j: (i // tiles_per_group, 0, j))
    return pl.BlockSpec((None, tm, tn), lambda i, j: (0, i, j))


def _norm_mod(x, g, scale, shift, rows_per_group, tm, out_dtype):
    m = x.shape[0]
    tpg = max(rows_per_group // tm, 1)
    g2 = g.reshape(1, D_MODEL)
    return pl.pallas_call(
        _norm_mod_kernel,
        grid=(m // tm, 1),
        in_specs=[pl.BlockSpec((tm, D_MODEL), lambda i, j: (i, 0)),
                  pl.BlockSpec((1, D_MODEL), lambda i, j: (0, 0)),
                  _row_spec(scale, tm, D_MODEL, tpg),
                  _row_spec(shift, tm, D_MODEL, tpg)],
        out_specs=pl.BlockSpec((tm, D_MODEL), lambda i, j: (i, 0)),
        out_shape=jax.ShapeDtypeStruct((m, D_MODEL), out_dtype),
        compiler_params=_params("parallel", "arbitrary"),
        name="norm_mod",
    )(x, g2, scale, shift)


def _final_norm(x, g, tm):
    m = x.shape[0]
    return pl.pallas_call(
        _norm_kernel,
        grid=(m // tm,),
        in_specs=[pl.BlockSpec((tm, D_MODEL), lambda i: (i, 0)),
                  pl.BlockSpec((1, D_MODEL), lambda i: (0, 0))],
        out_specs=pl.BlockSpec((tm, D_MODEL), lambda i: (i, 0)),
        out_shape=jax.ShapeDtypeStruct((m, D_MODEL), F32),
        compiler_params=_params("parallel"),
        name="final_norm",
    )(x, g.reshape(1, D_MODEL))


def _follow_first_tile(i, j, last):
    return jnp.where(i == 0, j, last)


def _mm_kernel(a_ref, as_ref, w_ref, o_ref, os_ref):
    w = w_ref[...].astype(BF16)
    o_ref[...] = jnp.dot(a_ref[...].astype(BF16), w, preferred_element_type=F32)

    @pl.when(pl.program_id(0) == 0)
    def _():
        os_ref[...] = jnp.dot(as_ref[...].astype(BF16), w, preferred_element_type=F32)


def _matmul(a, a_s, w, layer, tm, tn):
    m, k = a.shape
    ms = a_s.shape[0]
    n = w.shape[2]
    last = n // tn - 1
    return pl.pallas_call(
        _mm_kernel,
        grid=(m // tm, n // tn),
        in_specs=[pl.BlockSpec((tm, k), lambda i, j: (i, 0)),
                  pl.BlockSpec((ms, k), lambda i, j: (0, 0)),
                  pl.BlockSpec((None, k, tn), lambda i, j: (layer, 0, j))],
        out_specs=[pl.BlockSpec((tm, tn), lambda i, j: (i, j)),
                   pl.BlockSpec((ms, tn), lambda i, j: (0, _follow_first_tile(i, j, last)))],
        out_shape=[jax.ShapeDtypeStruct((m, n), F32), jax.ShapeDtypeStruct((ms, n), F32)],
        compiler_params=_params("arbitrary", "arbitrary"),
        name="matmul",
    )(a, a_s, w)


def _in_proj_kernel(a_ref, as_ref, w_ref, kin_ref, vin_ref, o_ref, k_ref, v_ref, os_ref, ks_ref, vs_ref, *, kv_tiles):
    del kin_ref, vin_ref
    j = pl.program_id(1)
    is_k = (j >= kv_tiles) & (j < 2 * kv_tiles)
    is_v = (j >= 2 * kv_tiles) & (j < 3 * kv_tiles)
    w = w_ref[...].astype(BF16)
    acc = jnp.dot(a_ref[...].astype(BF16), w, preferred_element_type=F32)
    o_ref[...] = acc

    @pl.when(is_k)
    def _():
        k_ref[...] = acc

    @pl.when(is_v)
    def _():
        v_ref[...] = acc

    @pl.when(pl.program_id(0) == 0)
    def _():
        acc_s = jnp.dot(as_ref[...].astype(BF16), w, preferred_element_type=F32)
        os_ref[...] = acc_s

        @pl.when(is_k)
        def _():
            ks_ref[...] = acc_s

        @pl.when(is_v)
        def _():
            vs_ref[...] = acc_s


def _in_proj(a, a_s, w, k_all, v_all, layer, tm, tn):
    m, k = a.shape
    ms = a_s.shape[0]
    n = w.shape[2]
    last = n // tn - 1
    kv_tiles = D_ATTN // tn
    kv_col = lambda j, first: jnp.clip(j - first, 0, kv_tiles - 1)

    def kv_spec(first):
        return pl.BlockSpec((None, tm, tn), lambda i, j: (layer, i, kv_col(j, first)))

    def kv_s_spec(first):
        return pl.BlockSpec((ms, tn), lambda i, j: (0, kv_col(_follow_first_tile(i, j, last), first)))

    return pl.pallas_call(
        functools.partial(_in_proj_kernel, kv_tiles=kv_tiles),
        grid=(m // tm, n // tn),
        in_specs=[pl.BlockSpec((tm, k), lambda i, j: (i, 0)),
                  pl.BlockSpec((ms, k), lambda i, j: (0, 0)),
                  pl.BlockSpec((None, k, tn), lambda i, j: (layer, 0, j)),
                  pl.BlockSpec(memory_space=pl.ANY),
                  pl.BlockSpec(memory_space=pl.ANY)],
        out_specs=[pl.BlockSpec((tm, tn), lambda i, j: (i, j)), kv_spec(kv_tiles), kv_spec(2 * kv_tiles),
                   pl.BlockSpec((ms, tn), lambda i, j: (0, _follow_first_tile(i, j, last))),
                   kv_s_spec(kv_tiles), kv_s_spec(2 * kv_tiles)],
        out_shape=[jax.ShapeDtypeStruct((m, n), F32), jax.ShapeDtypeStruct(k_all.shape, F32),
                   jax.ShapeDtypeStruct(v_all.shape, F32), jax.ShapeDtypeStruct((ms, n), F32),
                   jax.ShapeDtypeStruct((ms, D_ATTN), F32), jax.ShapeDtypeStruct((ms, D_ATTN), F32)],
        input_output_aliases={3: 1, 4: 2},
        compiler_params=_params("arbitrary", "arbitrary"),
        name="in_proj",
    )(a, a_s, w, k_all, v_all)


def _mm2_res_kernel(a1_ref, a2_ref, a1s_ref, a2s_ref, w1_ref, w2_ref, x_ref, gate_ref, xs_ref, gates_ref,
                    o_ref, os_ref):
    w1 = w1_ref[...].astype(BF16)
    w2 = w2_ref[...].astype(BF16)
    acc = jnp.dot(a1_ref[...].astype(BF16), w1, preferred_element_type=F32)
    acc = acc + jnp.dot(a2_ref[...].astype(BF16), w2, preferred_element_type=F32)
    o_ref[...] = x_ref[...] + gate_ref[...] * acc

    @pl.when(pl.program_id(0) == 0)
    def _():
        acc_s = jnp.dot(a1s_ref[...].astype(BF16), w1, preferred_element_type=F32)
        acc_s = acc_s + jnp.dot(a2s_ref[...].astype(BF16), w2, preferred_element_type=F32)
        os_ref[...] = xs_ref[...] + gates_ref[...] * acc_s


def _matmul_concat_res(a1, a2, a1_s, a2_s, w, layer, tm, tn, x, gate, x_s, gate_s, rows_per_group):
    m, k1 = a1.shape
    k2 = a2.shape[1]
    ms = a1_s.shape[0]
    assert k1 == k2 and w.shape[1] == k1 + k2
    n = w.shape[2]
    last = n // tn - 1
    tpg = max(rows_per_group // tm, 1)
    s_col = lambda i, j: (0, _follow_first_tile(i, j, last))
    return pl.pallas_call(
        _mm2_res_kernel,
        grid=(m // tm, n // tn),
        in_specs=[pl.BlockSpec((tm, k1), lambda i, j: (i, 0)),
                  pl.BlockSpec((tm, k2), lambda i, j: (i, 0)),
                  pl.BlockSpec((ms, k1), lambda i, j: (0, 0)),
                  pl.BlockSpec((ms, k2), lambda i, j: (0, 0)),
                  pl.BlockSpec((None, k1, tn), lambda i, j: (layer, 0, j)),
                  pl.BlockSpec((None, k2, tn), lambda i, j: (layer, 1, j)),
                  pl.BlockSpec((tm, tn), lambda i, j: (i, j)),
                  _row_spec(gate, tm, tn, tpg),
                  pl.BlockSpec((ms, tn), s_col),
                  pl.BlockSpec((None, ms, tn), lambda i, j: (0,) + s_col(i, j))],
        out_specs=[pl.BlockSpec((tm, tn), lambda i, j: (i, j)), pl.BlockSpec((ms, tn), s_col)],
        out_shape=[jax.ShapeDtypeStruct((m, n), F32), jax.ShapeDtypeStruct((ms, n), F32)],
        compiler_params=_params("arbitrary", "arbitrary"),
        name="matmul_concat",
    )(a1, a2, a1_s, a2_s, w, w, x, gate, x_s, gate_s)


ATTN_GROUP = 8


def _attn_prompt_kernel(q_ref, k_ref, v_ref, o_ref, o_sc, lse_sc):
    seq = q_ref.shape[0]
    nt = (((1,), (1,)), ((), ()))
    row = lax.broadcasted_iota(jnp.int32, (ATTN_BLOCK, 2 * ATTN_BLOCK), 0)
    col = lax.broadcasted_iota(jnp.int32, (ATTN_BLOCK, 2 * ATTN_BLOCK), 1)
    band_mask = (col >= row) & (col <= row + ATTN_BLOCK)
    causal_mask = (lax.broadcasted_iota(jnp.int32, (ATTN_BLOCK, ATTN_BLOCK), 1)
                   <= lax.broadcasted_iota(jnp.int32, (ATTN_BLOCK, ATTN_BLOCK), 0))

    def rows_of(start, n, dil):
        return pl.ds(start, n, stride=dil) if dil > 1 else pl.ds(start, n)

    def run_group(p, dil, units):
        qs, ks, vs, masks = [], [], [], []
        for start, has_prev in units:
            nk = 2 * ATTN_BLOCK if has_prev else ATTN_BLOCK
            kstart = start - ATTN_BLOCK * dil if has_prev else start
            qs.append(q_ref[rows_of(start, ATTN_BLOCK, dil), :].astype(BF16))
            ks.append(k_ref[rows_of(kstart, nk, dil), :].astype(BF16))
            v = v_ref[rows_of(kstart, nk, dil), :].astype(BF16)
            vs.append(jnp.concatenate([v, jnp.ones((nk, HEAD_DIM), BF16)], axis=-1))
            masks.append(band_mask if has_prev else causal_mask)
        ss = [jnp.where(mk, lax.dot_general(q, k, nt, preferred_element_type=F32) * ATTN_SCALE, NEG_INF)
              for q, k, mk in zip(qs, ks, masks)]
        ms = [jnp.max(s, axis=-1, keepdims=True) for s in ss]
        ps = [jnp.exp(s - m).astype(BF16) for s, m in zip(ss, ms)]
        rs = [jnp.dot(pr, v, preferred_element_type=F32) for pr, v in zip(ps, vs)]
        for (start, _), r, m in zip(units, rs, ms):
            acc, den = r[:, :HEAD_DIM], r[:, HEAD_DIM:]
            rows = rows_of(start, ATTN_BLOCK, dil)
            o_sc[p, rows, :] = acc / den
            lse_sc[p, rows, :] = m + jnp.log(den)

    for p, dil in enumerate(DILATIONS):
        nblk = seq // dil // ATTN_BLOCK
        units = [(r + blk * ATTN_BLOCK * dil, blk > 0) for blk in range(nblk) for r in range(dil)]
        for g0 in range(0, len(units), ATTN_GROUP):
            run_group(p, dil, units[g0:g0 + ATTN_GROUP])

    l0, l1, l2 = lse_sc[0], lse_sc[1], lse_sc[2]
    mx = jnp.maximum(jnp.maximum(l0, l1), l2)
    w0, w1, w2 = jnp.exp(l0 - mx), jnp.exp(l1 - mx), jnp.exp(l2 - mx)
    out = (w0 * o_sc[0] + w1 * o_sc[1] + w2 * o_sc[2]) / (w0 + w1 + w2)
    o_ref[...] = out.astype(o_ref.dtype)


def _attn_prompt(proj, batch, seq):
    blk = (seq, HEAD_DIM)
    return pl.pallas_call(
        _attn_prompt_kernel,
        grid=(batch, N_HEADS),
        in_specs=[pl.BlockSpec(blk, lambda b, h: (b, h)),
                  pl.BlockSpec(blk, lambda b, h: (b, N_HEADS + h)),
                  pl.BlockSpec(blk, lambda b, h: (b, 2 * N_HEADS + h))],
        out_specs=pl.BlockSpec(blk, lambda b, h: (b, h)),
        out_shape=jax.ShapeDtypeStruct((batch * seq, D_ATTN), BF16),
        scratch_shapes=[pltpu.VMEM((len(DILATIONS), seq, HEAD_DIM), F32),
                        pltpu.VMEM((len(DILATIONS), seq, HEAD_DIM), F32)],
        compiler_params=_params("parallel", "parallel"),
        name="attn_prompt",
    )(proj, proj, proj)


def _pattern_count(dist):
    cnt = jnp.zeros(dist.shape, F32)
    for dil in DILATIONS:
        ok = (dist >= 0) & ((dist & (dil - 1)) == 0) & (dist <= ATTN_BLOCK * dil)
        cnt = cnt + jnp.where(ok, 1.0, 0.0)
    return cnt


def _attn_sample_kernel(q_ref, kn_ref, vn_ref, kc_ref, vc_ref, o_ref, m_sc, l_sc, acc_sc, cnt_sc, *, cache_len, chunk):
    b = pl.program_id(0)
    c = pl.program_id(1)
    t_new = q_ref.shape[0]
    rows_n = t_new * N_HEADS
    nt = (((1,), (1,)), ((), ()))
    q = q_ref[...].reshape(rows_n, HEAD_DIM).astype(BF16)

    def pattern_weights(n, key_pos):
        ri = lax.broadcasted_iota(jnp.int32, (rows_n, n), 0)
        ci = lax.broadcasted_iota(jnp.int32, (rows_n, n), 1)
        same_head = (ri & (N_HEADS - 1)) == (ci & (N_HEADS - 1))
        dist = cache_len + (ri >> HEAD_SHIFT) - (key_pos + (ci >> HEAD_SHIFT))
        return jnp.where(same_head, _pattern_count(dist), 0.0)

    def scores(k2d, cnt):
        s = lax.dot_general(q, k2d.astype(BF16), nt, preferred_element_type=F32) * ATTN_SCALE
        return jnp.where(cnt > 0.0, s, NEG_INF)

    @pl.when(b == 0)
    def _():
        cnt_sc[c] = pattern_weights(chunk * N_HEADS, c * chunk)

    @pl.when(c == 0)
    def _():
        cnt = pattern_weights(rows_n, cache_len)
        s = scores(kn_ref[...].reshape(rows_n, HEAD_DIM), cnt)
        m = jnp.max(s, axis=-1, keepdims=True)
        p = cnt * jnp.exp(s - m)
        m_sc[...] = m
        l_sc[...] = jnp.sum(p, axis=-1, keepdims=True)
        acc_sc[...] = jnp.dot(p.astype(BF16), vn_ref[...].reshape(rows_n, HEAD_DIM).astype(BF16),
                              preferred_element_type=F32)

    cnt = cnt_sc[c]
    s = scores(kc_ref[...].reshape(chunk * N_HEADS, HEAD_DIM), cnt)
    m_old = m_sc[...]
    m_new = jnp.maximum(m_old, jnp.max(s, axis=-1, keepdims=True))
    alpha = jnp.exp(m_old - m_new)
    p = cnt * jnp.exp(s - m_new)
    l_sc[...] = alpha * l_sc[...] + jnp.sum(p, axis=-1, keepdims=True)
    acc_sc[...] = alpha * acc_sc[...] + jnp.dot(
        p.astype(BF16), vc_ref[...].reshape(chunk * N_HEADS, HEAD_DIM).astype(BF16), preferred_element_type=F32)
    m_sc[...] = m_new

    @pl.when(c == pl.num_programs(1) - 1)
    def _():
        o_ref[...] = (acc_sc[...] / l_sc[...]).reshape(t_new, N_HEADS, HEAD_DIM)


def _attn_sample(proj3, cache_k, cache_v, layer, batch, t_new):
    cache_len = cache_k.shape[2]
    chunk = CACHE_CHUNK
    rows_n = t_new * N_HEADS
    new_blk = (t_new, N_HEADS, HEAD_DIM)
    cache_blk = (None, None, chunk, N_HEADS, HEAD_DIM)
    return pl.pallas_call(
        functools.partial(_attn_sample_kernel, cache_len=cache_len, chunk=chunk),
        grid=(batch, cache_len // chunk),
        in_specs=[pl.BlockSpec(new_blk, lambda b, c: (b, 0, 0)),
                  pl.BlockSpec(new_blk, lambda b, c: (b, 1, 0)),
                  pl.BlockSpec(new_blk, lambda b, c: (b, 2, 0)),
                  pl.BlockSpec(cache_blk, lambda b, c: (layer, b, c, 0, 0)),
                  pl.BlockSpec(cache_blk, lambda b, c: (layer, b, c, 0, 0))],
        out_specs=pl.BlockSpec(new_blk, lambda b, c: (b, 0, 0)),
        out_shape=jax.ShapeDtypeStruct((batch * t_new, N_HEADS, HEAD_DIM), F32),
        scratch_shapes=[pltpu.VMEM((rows_n, 1), F32), pltpu.VMEM((rows_n, 1), F32),
                        pltpu.VMEM((rows_n, HEAD_DIM), F32),
                        pltpu.VMEM((cache_len // chunk, rows_n, chunk * N_HEADS), F32)],
        compiler_params=_params("arbitrary", "arbitrary"),
        name="attn_sample",
    )(proj3, proj3, proj3, cache_k, cache_v)


def _conv_kernel(ga_ref, gb_ref, st_ref, wdw_ref, bdw_ref, g_ref, b_ref, o_ref, so_ref, ubuf, cbuf, *, ts, lane_chunk):
    j = pl.program_id(1)
    first = STATE_PAD - CONV_STATE

    @pl.when(j == 0)
    def _():
        ubuf[0:STATE_PAD, :] = st_ref[...]
        ubuf[STATE_PAD + ts:STATE_PAD + ts + SUBLANES, :] = jnp.zeros((SUBLANES, D_CONV), F32)

    gb = gb_ref[...]
    ubuf[STATE_PAD:STATE_PAD + ts, :] = ga_ref[...] * jax.nn.sigmoid(gb)

    for c0 in range(0, D_CONV, lane_chunk):
        lanes = slice(c0, c0 + lane_chunk)
        acc = jnp.broadcast_to(bdw_ref[:, lanes], (ts, lane_chunk))
        for rho in range(SUBLANES):
            part = None
            for base in range(0, STATE_PAD + SUBLANES, SUBLANES):
                w = base + rho - first
                if 0 <= w < CONV_WIDTH:
                    term = ubuf[base:base + ts + SUBLANES, lanes] * wdw_ref[w:w + 1, lanes]
                    part = term if part is None else part + term
            acc = acc + part[rho:rho + ts, :]
        cbuf[:, lanes] = acc

    conv = cbuf[...]
    mu = jnp.mean(conv, axis=-1, keepdims=True)
    xc = conv - mu
    y = xc * lax.rsqrt(jnp.mean(xc * xc, axis=-1, keepdims=True) + NORM_EPS)
    y = y * g_ref[...] + b_ref[...]
    o_ref[...] = (y * jax.nn.sigmoid(y)).astype(o_ref.dtype)

    @pl.when(j == pl.num_programs(1) - 1)
    def _():
        so_ref[...] = ubuf[ts + first:ts + STATE_PAD, :]

    tail = ubuf[ts:ts + STATE_PAD, :]
    ubuf[0:STATE_PAD, :] = tail


def _conv_module(proj, state_pad, w_dw, b_dw, g_ln, b_ln, layer, batch, seq, ts, out_dtype):
    steps = seq // ts
    ga_col = 3 * D_ATTN // D_CONV
    vec = lambda a: a.reshape(a.shape[0], 1, D_CONV)
    vspec = pl.BlockSpec((None, 1, D_CONV), lambda b, j: (layer, 0, 0))
    return pl.pallas_call(
        functools.partial(_conv_kernel, ts=ts, lane_chunk=CONV_LANES),
        grid=(batch, steps),
        in_specs=[pl.BlockSpec((ts, D_CONV), lambda b, j: (b * steps + j, ga_col)),
                  pl.BlockSpec((ts, D_CONV), lambda b, j: (b * steps + j, ga_col + 1)),
                  pl.BlockSpec((None, STATE_PAD, D_CONV), lambda b, j: (b, 0, 0)),
                  pl.BlockSpec((None, CONV_WIDTH, D_CONV), lambda b, j: (layer, 0, 0)),
                  vspec, vspec, vspec],
        out_specs=[pl.BlockSpec((ts, D_CONV), lambda b, j: (b * steps + j, 0)),
                   pl.BlockSpec((None, CONV_STATE, D_CONV), lambda b, j: (b, 0, 0))],
        out_shape=[jax.ShapeDtypeStruct((batch * seq, D_CONV), out_dtype),
                   jax.ShapeDtypeStruct((batch, CONV_STATE, D_CONV), F32)],
        scratch_shapes=[pltpu.VMEM((STATE_PAD + ts + SUBLANES, D_CONV), F32), pltpu.VMEM((ts, D_CONV), F32)],
        compiler_params=_params("parallel", "arbitrary"),
        name="conv_module",
    )(proj, proj, state_pad, w_dw, vec(b_dw), vec(g_ln), vec(b_ln))


CAND_PAD = 1.0e9
TOKEN_UNROLL = 32
HEAD_GROUP = 8


def _cand_plan():
    tiles = [("a", 0, 0), ("a", 0, SUBLANES), ("a", 1, 0), ("b", 0, SUBLANES)] + [("b", b, 0) for b in range(5)]
    flat, seen = [], set()
    for kind, fixed, start in tiles:
        for r in range(SUBLANES):
            a, b = (fixed, start + r) if kind == "a" else (start + r, fixed)
            if (a + 1) * (b + 1) <= PEER_TOPK and (a, b) not in seen:
                seen.add((a, b))
                flat.append(float(a * PEER_TOPK + b))
            else:
                flat.append(CAND_PAD)
    need = {(a, b) for a in range(PEER_TOPK) for b in range(PEER_TOPK) if (a + 1) * (b + 1) <= PEER_TOPK}
    assert seen == need
    return tiles, flat


def _peer_topk_kernel(q_ref, keys_ref, flat_ref, g_ref, s_sc, v_sc, i_sc, cand_sc, cid_sc, top_sc,
                      e_sc, w_sc, i1t_sc, i2t_sc, wt_sc, *, tiles):
    tm = q_ref.shape[0]
    nt = (((1,), (1,)), ((), ()))
    iota_keys = lax.broadcasted_iota(jnp.int32, (N_KEYS, tm), 0).astype(F32)
    flat = flat_ref[...]
    valid = flat < CAND_PAD

    def head_group(hg, carry):
        for hh in range(HEAD_GROUP):
            h = hg * HEAD_GROUP + hh
            col = pl.multiple_of(h * PEER_QDIM, PEER_QDIM)
            for half in range(2):
                qh = q_ref[:, pl.ds(col + half * N_KEYS, N_KEYS)].astype(BF16)
                kh = keys_ref[h, half].astype(BF16)
                s_sc[hh * 2 + half] = lax.dot_general(kh, qh, nt, preferred_element_type=F32)

        def key_step(k, c):
            for ch in range(2 * HEAD_GROUP):
                s = s_sc[ch]
                m = jnp.max(s, axis=0, keepdims=True)
                ix = jnp.min(jnp.where(s == m, iota_keys, float(N_KEYS)), axis=0, keepdims=True)
                v_sc[ch, pl.ds(k, 1), :] = m
                i_sc[ch, pl.ds(k, 1), :] = ix
                s_sc[ch] = jnp.where(iota_keys == ix, NEG_INF, s)
            return c

        lax.fori_loop(0, PEER_TOPK, key_step, 0)

        for hh in range(HEAD_GROUP):
            v1, v2, i1, i2 = v_sc[2 * hh], v_sc[2 * hh + 1], i_sc[2 * hh], i_sc[2 * hh + 1]
            sums, ids = [], []
            for kind, fixed, start in tiles:
                one, rng = slice(fixed, fixed + 1), slice(start, start + SUBLANES)
                ra, rb = (one, rng) if kind == "a" else (rng, one)
                sums.append(v1[ra, :] + v2[rb, :])
                ids.append(i1[ra, :] * float(N_KEYS) + i2[rb, :])
            cand_sc[hh] = jnp.where(valid, jnp.concatenate(sums, axis=0), NEG_INF)
            cid_sc[hh] = jnp.concatenate(ids, axis=0)

        def cand_step(k, c):
            for hh in range(HEAD_GROUP):
                cnd = cand_sc[hh]
                m = jnp.max(cnd, axis=0, keepdims=True)
                pos = jnp.min(jnp.where(cnd == m, flat, 2.0 * CAND_PAD), axis=0, keepdims=True)
                hit = flat == pos
                top_sc[hh, pl.ds(k, 1), :] = m
                e_sc[pl.ds((hg * HEAD_GROUP + hh) * PEER_TOPK + k, 1), :] = jnp.max(
                    jnp.where(hit, cid_sc[hh], -1.0), axis=0, keepdims=True)
                cand_sc[hh] = jnp.where(hit, NEG_INF, cnd)
            return c

        lax.fori_loop(0, PEER_TOPK, cand_step, 0)

        for hh in range(HEAD_GROUP):
            top = top_sc[hh]
            ex = jnp.exp(top - top[0:1, :])
            out_rows = pl.ds(pl.multiple_of((hg * HEAD_GROUP + hh) * PEER_TOPK, PEER_TOPK), PEER_TOPK)
            w_sc[out_rows, :] = ex / jnp.sum(ex, axis=0, keepdims=True)
        return carry

    lax.fori_loop(0, PEER_HEADS // HEAD_GROUP, head_group, 0)

    e = e_sc[...]
    i1 = jnp.floor(e * (1.0 / N_KEYS))
    i1t_sc[...] = jnp.transpose(i1)
    i2t_sc[...] = jnp.transpose(e - i1 * float(N_KEYS))
    wt_sc[...] = jnp.transpose(w_sc[...])

    sub = lax.broadcasted_iota(jnp.int32, (N_KEYS, PEER_HEADS * PEER_TOPK), 0).astype(F32)

    def token(t, carry):
        i1row = i1t_sc[pl.ds(t, 1), :]
        i2row = i2t_sc[pl.ds(t, 1), :]
        wrow = wt_sc[pl.ds(t, 1), :]
        lmat = jnp.where(sub == i1row, wrow, 0.0).astype(BF16)
        rmat = jnp.where(sub == i2row, 1.0, 0.0).astype(BF16)
        g_ref[t] = lax.dot_general(lmat, rmat, nt, preferred_element_type=F32)
        return carry

    lax.fori_loop(0, tm, token, 0, unroll=TOKEN_UNROLL)


def _peer_topk(q, sub_keys, layer):
    t = q.shape[0]
    tm = LANES
    slots = PEER_HEADS * PEER_TOPK
    tiles, flat = _cand_plan()
    ncand = len(flat)
    flat_arr = jnp.broadcast_to(jnp.asarray(flat, F32)[:, None], (ncand, tm))
    sc = lambda *shape: pltpu.VMEM(shape + (tm,), F32)
    return pl.pallas_call(
        functools.partial(_peer_topk_kernel, tiles=tiles),
        grid=(t // tm,),
        in_specs=[pl.BlockSpec((tm, PEER_HEADS * PEER_QDIM), lambda i: (i, 0)),
                  pl.BlockSpec((None, PEER_HEADS, 2, N_KEYS, PEER_QDIM // 2), lambda i: (layer, 0, 0, 0, 0)),
                  pl.BlockSpec((ncand, tm), lambda i: (0, 0))],
        out_specs=pl.BlockSpec((tm, N_KEYS, N_KEYS), lambda i: (i, 0, 0)),
        out_shape=jax.ShapeDtypeStruct((t, N_KEYS, N_KEYS), F32),
        scratch_shapes=[sc(2 * HEAD_GROUP, N_KEYS), sc(2 * HEAD_GROUP, PEER_TOPK), sc(2 * HEAD_GROUP, PEER_TOPK),
                        sc(HEAD_GROUP, ncand), sc(HEAD_GROUP, ncand), sc(HEAD_GROUP, PEER_TOPK),
                        sc(slots), sc(slots),
                        pltpu.VMEM((tm, slots), F32), pltpu.VMEM((tm, slots), F32), pltpu.VMEM((tm, slots), F32)],
        compiler_params=_params("parallel"),
        name="peer_topk",
    )(q, sub_keys, flat_arr)


def _gelu(x):
    return 0.5 * x * (1.0 + lax.erf(x * (2.0 ** -0.5)))


G_ROWS = 8


def _gated_gelu_tile(h_bf16, u_rows, g2, first_row, n_rows, n_sub):
    nt = (((1,), (1,)), ((), ()))
    act = _gelu(lax.dot_general(h_bf16, u_rows.astype(BF16), nt, preferred_element_type=F32))
    return [(g2[pl.ds(first_row + c, n_rows, stride=G_ROWS), :] * act[:, c * N_KEYS:(c + 1) * N_KEYS]).astype(BF16)
            for c in range(n_sub)]


def _peer_up_kernel(h_ref, hs_ref, u_ref, g_ref, gs_ref, o_ref, os_ref, *, nsub):
    tm, ms = h_ref.shape[0], hs_ref.shape[0]
    g2 = g_ref.reshape(tm * G_ROWS, N_KEYS)
    gs2 = gs_ref.reshape(ms * G_ROWS, N_KEYS)
    first_row = (pl.program_id(1) % (G_ROWS // nsub)) * nsub
    hb = h_ref[...].astype(BF16)
    half = nsub // 2
    for c0 in range(0, nsub, half):
        pieces = _gated_gelu_tile(hb, u_ref[c0 * N_KEYS:(c0 + half) * N_KEYS, :], g2, first_row + c0, tm, half)
        for c, piece in enumerate(pieces):
            o_ref[:, (c0 + c) * N_KEYS:(c0 + c + 1) * N_KEYS] = piece

    @pl.when(pl.program_id(0) == 0)
    def _():
        pieces = _gated_gelu_tile(hs_ref[...].astype(BF16), u_ref[...], gs2, first_row, ms, nsub)
        for c, piece in enumerate(pieces):
            os_ref[:, c * N_KEYS:(c + 1) * N_KEYS] = piece


def _peer_up(h, h_s, peer_u, g3, g3_s, layer, tm, tn):
    m, ms = h.shape[0], h_s.shape[0]
    nsub = tn // N_KEYS
    phases = G_ROWS // nsub
    last = N_EXPERTS // tn - 1
    return pl.pallas_call(
        functools.partial(_peer_up_kernel, nsub=nsub),
        grid=(m // tm, N_EXPERTS // tn),
        in_specs=[pl.BlockSpec((tm, D_MODEL), lambda i, j: (i, 0)),
                  pl.BlockSpec((ms, D_MODEL), lambda i, j: (0, 0)),
                  pl.BlockSpec((None, tn, D_MODEL), lambda i, j: (layer, j, 0)),
                  pl.BlockSpec((tm, G_ROWS, N_KEYS), lambda i, j: (i, j // phases, 0)),
                  pl.BlockSpec((ms, G_ROWS, N_KEYS), lambda i, j: (0, _follow_first_tile(i, j, last) // phases, 0))],
        out_specs=[pl.BlockSpec((tm, tn), lambda i, j: (i, j)),
                   pl.BlockSpec((ms, tn), lambda i, j: (0, _follow_first_tile(i, j, last)))],
        out_shape=[jax.ShapeDtypeStruct((m, N_EXPERTS), BF16), jax.ShapeDtypeStruct((ms, N_EXPERTS), BF16)],
        compiler_params=_params("arbitrary", "arbitrary"),
        name="peer_up",
    )(h, h_s, peer_u, g3, g3_s)


def _peer_down_kernel(w_ref, ws_ref, v_ref, x_ref, gate_ref, xs_ref, gates_ref, o_ref, os_ref):
    k = pl.program_id(2)
    first, last = k == 0, k == pl.num_programs(2) - 1

    def accumulate(lhs_ref, res_ref, gate, acc_ref):
        @pl.when(first)
        def _():
            acc_ref[...] = jnp.zeros_like(acc_ref)

        acc_ref[...] += jnp.dot(lhs_ref[...], v_ref[...].astype(BF16), preferred_element_type=F32)

        @pl.when(last)
        def _():
            acc_ref[...] = res_ref[...] + gate[...] * acc_ref[...]

    accumulate(w_ref, x_ref, gate_ref, o_ref)

    @pl.when(pl.program_id(0) == 0)
    def _():
        accumulate(ws_ref, xs_ref, gates_ref, os_ref)


def _peer_down(w, w_s, peer_v, x, gate, x_s, gate_s, layer, rows_per_group, tm, tn, tk):
    m, ms = w.shape[0], w_s.shape[0]
    tpg = max(rows_per_group // tm, 1)
    last_j, last_k = D_MODEL // tn - 1, N_EXPERTS // tk - 1
    s_col = lambda i, j, k: (0, _follow_first_tile(i, j, last_j))
    return pl.pallas_call(
        _peer_down_kernel,
        grid=(m // tm, D_MODEL // tn, N_EXPERTS // tk),
        in_specs=[pl.BlockSpec((tm, tk), lambda i, j, k: (i, k)),
                  pl.BlockSpec((ms, tk), lambda i, j, k: (0, _follow_first_tile(i, k, last_k))),
                  pl.BlockSpec((None, tk, tn), lambda i, j, k: (layer, k, j)),
                  pl.BlockSpec((tm, tn), lambda i, j, k: (i, j)),
                  pl.BlockSpec((None, 1, tn), lambda i, j, k: (i // tpg, 0, j)),
                  pl.BlockSpec((ms, tn), s_col),
                  pl.BlockSpec((None, ms, tn), lambda i, j, k: (0,) + s_col(i, j, k))],
        out_specs=[pl.BlockSpec((tm, tn), lambda i, j, k: (i, j)), pl.BlockSpec((ms, tn), s_col)],
        out_shape=[jax.ShapeDtypeStruct((m, D_MODEL), F32), jax.ShapeDtypeStruct((ms, D_MODEL), F32)],
        compiler_params=_params("arbitrary", "arbitrary", "arbitrary"),
        name="peer_down",
    )(w, w_s, peer_v, x, gate, x_s, gate_s)


def _layer(xp, xs, mods_p, mods_s, seq, attn_p_fn, attn_s_fn, conv_p_fn, conv_s_fn, k_all, v_all, layer, weights):
    (g_mix, g_ffn, w_in, w_out, w_peer_q, peer_sub_keys, peer_u, peer_v) = weights
    shift1, scale1, gate1, shift2, scale2, gate2 = mods_p
    shift1s, scale1s, gate1s, shift2s, scale2s, gate2s = mods_s
    ms = xs.shape[0]

    hp = _norm_mod(xp, g_mix[layer], scale1, shift1, seq, TM_NORM, BF16)
    hs = _norm_mod(xs, g_mix[layer], scale1s, shift1s, ms, ms, F32)
    proj_p, k_all, v_all, proj_s, k_s, v_s = _in_proj(hp, hs, w_in, k_all, v_all, layer, TM, TN)
    conv_p, cstate_p = conv_p_fn(proj_p)
    conv_s, cstate_s = conv_s_fn(proj_s)
    xp, xs = _matmul_concat_res(attn_p_fn(proj_p), conv_p, attn_s_fn(proj_s), conv_s, w_out, layer, TM, TN,
                                xp, gate1, xs, gate1s, seq)

    h2p = _norm_mod(xp, g_ffn[layer], scale2, shift2, seq, TM_NORM, BF16)
    h2s = _norm_mod(xs, g_ffn[layer], scale2s, shift2s, ms, ms, F32)
    qp, qs = _matmul(h2p, h2s, w_peer_q, layer, TM, TN)
    g3p = _peer_topk(qp, peer_sub_keys, layer)
    g3s = _peer_topk(jnp.pad(qs, ((0, (-ms) % LANES), (0, 0))), peer_sub_keys, layer)
    wp, ws = _peer_up(h2p, h2s, peer_u, g3p, g3s, layer, TM, TN)
    xp, xs = _peer_down(wp, ws, peer_v, xp, gate2, xs, gate2s, layer, seq, TM, DOWN_TN, DOWN_TK)
    return xp, xs, k_all, v_all, k_s, v_s, cstate_p, cstate_s


def kernel(x_prompt, x_sample, cache_k, cache_v, state_conv, c_prompt, c_sample, w_mod, b_mod, g_mix, g_ffn, w_in,
           w_dw, b_dw, g_conv_ln, b_conv_ln, w_out, w_peer_q, peer_sub_keys, peer_u, peer_v, g_final):
    depth = w_mod.shape[0]
    batch, seq, _ = x_prompt.shape
    dbatch, dseq, _ = x_sample.shape
    mp, ms = batch * seq, dbatch * dseq

    c_all = jnp.concatenate([c_prompt, c_sample], axis=0)
    c_pad = (-c_all.shape[0]) % 8
    mod = _modulation(jnp.pad(c_all, ((0, c_pad), (0, 0))), w_mod, b_mod)

    xp = x_prompt.reshape(mp, D_MODEL)
    xs = x_sample.reshape(ms, D_MODEL)
    weights = (g_mix, g_ffn, w_in, w_out, w_peer_q, peer_sub_keys, peer_u, peer_v)
    zero_state = jnp.zeros((batch, STATE_PAD, D_CONV), F32)
    k_all = jnp.zeros((depth, mp, D_ATTN), F32)
    v_all = jnp.zeros((depth, mp, D_ATTN), F32)
    outs = {name: [] for name in ("cp", "ks", "vs", "cs")}

    for l in range(depth):
        mod_p = mod[l, :batch].reshape(batch, N_MOD, 1, D_MODEL)
        mods_p = tuple(mod_p[:, i] for i in range(N_MOD))
        mod_s = mod[l, batch:batch + dbatch].reshape(dbatch, N_MOD, D_MODEL)
        mods_s = tuple(jnp.repeat(mod_s[:, i], dseq, axis=0)[None] for i in range(N_MOD))

        conv_args = dict(w_dw=w_dw, b_dw=b_dw, g_ln=g_conv_ln, b_ln=b_conv_ln, layer=l)
        conv_p = functools.partial(_conv_module, state_pad=zero_state, batch=batch, seq=seq, ts=CONV_ROWS, out_dtype=BF16,
                                   **conv_args)
        state_s = jnp.pad(state_conv[l], ((0, 0), (STATE_PAD - CONV_STATE, 0), (0, 0)))
        conv_s = functools.partial(_conv_module, state_pad=state_s, batch=dbatch, seq=dseq, ts=dseq, out_dtype=F32,
                                   **conv_args)

        def attn_s(proj, l=l):
            o = _attn_sample(proj.reshape(ms, D_IN // HEAD_DIM, HEAD_DIM), cache_k, cache_v, l, dbatch, dseq)
            return o.reshape(ms, D_ATTN)

        xp, xs, k_all, v_all, k_s, v_s, cstate_p, cstate_s = _layer(
            xp, xs, mods_p, mods_s, seq, functools.partial(_attn_prompt, batch=batch, seq=seq), attn_s,
            conv_p, conv_s, k_all, v_all, l, weights)

        outs["cp"].append(cstate_p)
        outs["ks"].append(k_s.reshape(dbatch, dseq, N_HEADS, HEAD_DIM))
        outs["vs"].append(v_s.reshape(dbatch, dseq, N_HEADS, HEAD_DIM))
        outs["cs"].append(cstate_s)

    keep = min(cache_k.shape[2], seq)
    k_prompt = k_all.reshape(depth, batch, seq, N_HEADS, HEAD_DIM)[:, :, seq - keep:]
    v_prompt = v_all.reshape(depth, batch, seq, N_HEADS, HEAD_DIM)[:, :, seq - keep:]
    y_prompt = _final_norm(xp, g_final, TM_NORM).reshape(batch, seq, D_MODEL)
    y_sample = _final_norm(xs, g_final, ms).reshape(dbatch, dseq, D_MODEL)
    return (y_prompt, y_sample, k_prompt, v_prompt, jnp.stack(outs["cp"]),
            jnp.stack(outs["ks"]), jnp.stack(outs["vs"]), jnp.stack(outs["cs"]))
```
